```python
import math
import jax, jax.numpy as jnp
from jax import lax
import numpy as np

D_MODEL = 4096
BATCH = 1
SEQ = 16384
DEPTH = 1
DEC_BATCH = 4
DEC_SEQ = 2048
PAST_LEN = 128

ATTN_WIDTH = D_MODEL // 2
REC_WIDTH = D_MODEL - ATTN_WIDTH
MIX_WIDTH = ATTN_WIDTH + REC_WIDTH
V_HEAD_DIM = 128
N_ATTN_HEADS = ATTN_WIDTH // V_HEAD_DIM
QK_HALF_DIM = V_HEAD_DIM // 2
N_REC_BLOCKS = 16
REC_BLOCK = REC_WIDTH // N_REC_BLOCKS
CONV_WIDTH = 4
RG_C = 8.0
N_EXPERTS = 16
EC_CAPACITY_FACTOR = 2
D_FF_EXPERT = D_MODEL
Q_BLOCK = 128
NORM_EPS = 1e-6
IN_WIDTH = 3 * ATTN_WIDTH + 2 * REC_WIDTH

kernel_name = "hymba_diffattn_rglru_ec_moe_encoder"


def rms_norm(x, g):
    xf = x.astype(jnp.float32)
    inv = lax.rsqrt(jnp.mean(xf * xf, axis=-1, keepdims=True) + NORM_EPS)
    return (xf * inv * g.astype(jnp.float32)).astype(x.dtype)


def alibi_slopes(n_heads):
    return 2.0 ** (-8.0 * jnp.arange(1, n_heads + 1, dtype=jnp.float32) / n_heads)


def lambda_init_for_layer(layer):
    return 0.8 - 0.6 * math.exp(-0.3 * layer)


def diff_attention(q, k, v, lam):
    B, S = q.shape[0], q.shape[1]
    n_blk = S // Q_BLOCK
    scale = QK_HALF_DIM ** -0.5
    slopes = alibi_slopes(N_ATTN_HEADS)
    k_pos = jnp.arange(S, dtype=jnp.float32)
    q_blocks = q.reshape(B, n_blk, Q_BLOCK, N_ATTN_HEADS, 2, QK_HALF_DIM).transpose(1, 0, 2, 3, 4, 5)
    starts = jnp.arange(n_blk, dtype=jnp.float32) * Q_BLOCK

    def one_block(args):
        q_blk, start = args
        s = jnp.einsum('bqhcd,bkhcd->bhcqk', q_blk, k,
                       preferred_element_type=jnp.float32) * scale
        q_pos = start + jnp.arange(Q_BLOCK, dtype=jnp.float32)
        dist = jnp.abs(q_pos[:, None] - k_pos[None, :])
        s = s - slopes[None, :, None, None, None] * dist[None, None, None]
        p = jax.nn.softmax(s, axis=-1)
        w = p[:, :, 0] - lam * p[:, :, 1]
        return jnp.einsum('bhqk,bkhe->bqhe', w.astype(v.dtype), v)

    out = lax.map(one_block, (q_blocks, starts))
    return out.transpose(1, 0, 2, 3, 4).reshape(B, S, N_ATTN_HEADS, V_HEAD_DIM)


def centred_depthwise_conv(x, w, b):
    S = x.shape[1]
    left = CONV_WIDTH // 2
    xp = jnp.pad(x, ((0, 0), (left, CONV_WIDTH - 1 - left), (0, 0)))
    y = b
    for j in range(CONV_WIDTH):
        y = y + xp[:, j:j + S] * w[j]
    return y


def linear_scan_combine(e1, e2):
    a1, b1 = e1
    a2, b2 = e2
    return a1 * a2, a2 * b1 + b2


def bidir_rg_lru(x, wa, ba, wx, bx, lam):
    B, S, C = x.shape
    xg = x.reshape(B, S, N_REC_BLOCKS, REC_BLOCK)
    r_pre = jnp.einsum('bsgc,ngce->nbsge', xg, wa).reshape(2, B, S, C) + ba[:, None, None, :]
    i_pre = jnp.einsum('bsgc,ngce->nbsge', xg, wx).reshape(2, B, S, C) + bx[:, None, None, :]
    r = jax.nn.sigmoid(r_pre.astype(jnp.float32))
    i = jax.nn.sigmoid(i_pre.astype(jnp.float32))
    log_a = -RG_C * r * jax.nn.softplus(-lam.astype(jnp.float32))[:, None, None, :]
    a = jnp.exp(log_a)
    u = jnp.sqrt(-jnp.expm1(2.0 * log_a)) * (i * x.astype(jnp.float32)[None])
    a = jnp.stack([a[0], jnp.flip(a[1], axis=1)])
    u = jnp.stack([u[0], jnp.flip(u[1], axis=1)])
    _, h = lax.associative_scan(linear_scan_combine, (a, u), axis=2)
    return (h[0] + jnp.flip(h[1], axis=1)).astype(x.dtype)


def hybrid_mixer(xn, layer, w_in, lambda_q1, lambda_k1, lambda_q2, lambda_k2, attn_subln_g,
                 conv_w, conv_b, rg_wa, rg_ba, rg_wx, rg_bx, rg_lambda, rec_norm_g, w_out):
    B, S, _ = xn.shape
    proj = jnp.einsum('bsd,de->bse', xn, w_in)
    q, k, v, xr, gr = jnp.split(
        proj, [ATTN_WIDTH, 2 * ATTN_WIDTH, 3 * ATTN_WIDTH, 3 * ATTN_WIDTH + REC_WIDTH], axis=-1)
    q = q.reshape(B, S, N_ATTN_HEADS, 2, QK_HALF_DIM)
    k = k.reshape(B, S, N_ATTN_HEADS, 2, QK_HALF_DIM)
    v = v.reshape(B, S, N_ATTN_HEADS, V_HEAD_DIM)
    lam_init = lambda_init_for_layer(layer)
    lam = (jnp.exp(jnp.sum(lambda_q1.astype(jnp.float32) * lambda_k1.astype(jnp.float32)))
           - jnp.exp(jnp.sum(lambda_q2.astype(jnp.float32) * lambda_k2.astype(jnp.float32)))
           + lam_init)
    attn = diff_attention(q, k, v, lam)
    attn = (rms_norm(attn, attn_subln_g) * (1.0 - lam_init)).reshape(B, S, ATTN_WIDTH)
    xr = centred_depthwise_conv(xr, conv_w, conv_b)
    rec = bidir_rg_lru(xr, rg_wa, rg_ba, rg_wx, rg_bx, rg_lambda)
    rec = rms_norm(rec * jax.nn.gelu(gr, approximate=True), rec_norm_g)
    mixed = jnp.concatenate([attn, rec], axis=-1)
    return jnp.einsum('bse,ed->bsd', mixed, w_out)


def expert_choice_moe(xn, router_w, exp_w1, exp_w3, exp_w2):
    B, S, D = xn.shape
    n_tok = B * S
    cap = (EC_CAPACITY_FACTOR * n_tok) // N_EXPERTS
    xf = xn.reshape(n_tok, D)
    aff = jax.nn.softmax(jnp.einsum('nd,de->ne', xf, router_w,
                                    preferred_element_type=jnp.float32), axis=-1)
    gates, idx = lax.top_k(aff.T, cap)
    xe = jnp.take(xf, idx, axis=0)
    h = jax.nn.silu(jnp.einsum('ecd,edf->ecf', xe, exp_w1)) * jnp.einsum('ecd,edf->ecf', xe, exp_w3)
    ye = jnp.einsum('ecf,efd->ecd', h, exp_w2) * gates[..., None].astype(xn.dtype)
    y = jnp.zeros_like(xf).at[idx.reshape(-1)].add(ye.reshape(-1, D))
    return y.reshape(B, S, D)


def trunk(x, norm_mix_g, w_in, lambda_q1, lambda_k1, lambda_q2, lambda_k2, attn_subln_g,
          conv_w, conv_b, rg_wa, rg_ba, rg_wx, rg_bx, rg_lambda, rec_norm_g, w_out,
          norm_ffn_g, router_w, exp_w1, exp_w3, exp_w2, final_norm_g):
    h = x
    for l in range(DEPTH):
        h = h + hybrid_mixer(rms_norm(h, norm_mix_g[l]), l, w_in[l], lambda_q1[l], lambda_k1[l],
                             lambda_q2[l], lambda_k2[l], attn_subln_g[l], conv_w[l], conv_b[l],
                             rg_wa[l], rg_ba[l], rg_wx[l], rg_bx[l], rg_lambda[l],
                             rec_norm_g[l], w_out[l])
        h = h + expert_choice_moe(rms_norm(h, norm_ffn_g[l]), router_w[l],
                                  exp_w1[l], exp_w3[l], exp_w2[l])
    return rms_norm(h, final_norm_g)


def setup_inputs(seed: int = 0) -> dict:
    key = jax.random.key(seed)
    ks = jax.random.split(key, 24)
    f32 = jnp.float32

    def nrm(k, shape, scale):
        return jax.random.normal(k, shape, f32) * scale

    a_c = jax.random.uniform(ks[16], (DEPTH, 2, REC_WIDTH), f32, minval=0.9, maxval=0.999)
    a_base = a_c ** (1.0 / RG_C)
    rg_lambda = jnp.log(a_base) - jnp.log1p(-a_base)
    return {
        "x_prompt": nrm(ks[0], (BATCH, SEQ, D_MODEL), 1.0),
        "x_sample": nrm(ks[1], (DEC_BATCH, DEC_SEQ, D_MODEL), 1.0),
        "norm_mix_g": 1.0 + nrm(ks[2], (DEPTH, D_MODEL), 0.02),
        "w_in": nrm(ks[3], (DEPTH, D_MODEL, IN_WIDTH), D_MODEL ** -0.5),
        "lambda_q1": nrm(ks[4], (DEPTH, QK_HALF_DIM), 0.1),
        "lambda_k1": nrm(ks[5], (DEPTH, QK_HALF_DIM), 0.1),
        "lambda_q2": nrm(ks[6], (DEPTH, QK_HALF_DIM), 0.1),
        "lambda_k2": nrm(ks[7], (DEPTH, QK_HALF_DIM), 0.1),
        "attn_subln_g": 1.0 + nrm(ks[8], (DEPTH, V_HEAD_DIM), 0.02),
        "conv_w": nrm(ks[9], (DEPTH, CONV_WIDTH, REC_WIDTH), CONV_WIDTH ** -0.5),
        "conv_b": nrm(ks[10], (DEPTH, REC_WIDTH), 0.02),
        "rg_wa": nrm(ks[11], (DEPTH, 2, N_REC_BLOCKS, REC_BLOCK, REC_BLOCK), REC_BLOCK ** -0.5),
        "rg_ba": nrm(ks[12], (DEPTH, 2, REC_WIDTH), 0.02),
        "rg_wx": nrm(ks[13], (DEPTH, 2, N_REC_BLOCKS, REC_BLOCK, REC_BLOCK), REC_BLOCK ** -0.5),
        "rg_bx": nrm(ks[14], (DEPTH, 2, REC_WIDTH), 0.02),
        "rg_lambda": rg_lambda,
        "rec_norm_g": 1.0 + nrm(ks[15], (DEPTH, REC_WIDTH), 0.02),
        "w_out": nrm(ks[17], (DEPTH, MIX_WIDTH, D_MODEL), MIX_WIDTH ** -0.5),
        "norm_ffn_g": 1.0 + nrm(ks[18], (DEPTH, D_MODEL), 0.02),
        "router_w": nrm(ks[19], (DEPTH, D_MODEL, N_EXPERTS), D_MODEL ** -0.5),
        "exp_w1": nrm(ks[20], (DEPTH, N_EXPERTS, D_MODEL, D_FF_EXPERT), D_MODEL ** -0.5),
        "exp_w3": nrm(ks[21], (DEPTH, N_EXPERTS, D_MODEL, D_FF_EXPERT), D_MODEL ** -0.5),
        "exp_w2": nrm(ks[22], (DEPTH, N_EXPERTS, D_FF_EXPERT, D_MODEL), D_FF_EXPERT ** -0.5),
        "final_norm_g": 1.0 + nrm(ks[23], (D_MODEL,), 0.02),
    }


def reference(x_prompt, x_sample, norm_mix_g, w_in, lambda_q1, lambda_k1, lambda_q2, lambda_k2,
              attn_subln_g, conv_w, conv_b, rg_wa, rg_ba, rg_wx, rg_bx, rg_lambda, rec_norm_g,
              w_out, norm_ffn_g, router_w, exp_w1, exp_w3, exp_w2, final_norm_g):
    y_prompt = trunk(x_prompt, norm_mix_g, w_in, lambda_q1, lambda_k1, lambda_q2, lambda_k2,
                     attn_subln_g, conv_w, conv_b, rg_wa, rg_ba, rg_wx, rg_bx, rg_lambda,
                     rec_norm_g, w_out, norm_ffn_g, router_w, exp_w1, exp_w3, exp_w2, final_norm_g)
    y_sample = trunk(x_sample, norm_mix_g, w_in, lambda_q1, lambda_k1, lambda_q2, lambda_k2,
                     attn_subln_g, conv_w, conv_b, rg_wa, rg_ba, rg_wx, rg_bx, rg_lambda,
                     rec_norm_g, w_out, norm_ffn_g, router_w, exp_w1, exp_w3, exp_w2, final_norm_g)
    return (y_prompt, y_sample)
```

```python
import functools
import math

import jax
import jax.numpy as jnp
from jax import lax
from jax.experimental import pallas as pl
from jax.experimental.pallas import tpu as pltpu

V_HEAD_DIM = 128
QK_HALF_DIM = V_HEAD_DIM // 2
CONV_WIDTH = 4
RG_C = 8.0
EC_CAPACITY_FACTOR = 2
NORM_EPS = 1e-6
LANES = 128
SUBLANES = 8
VMEM_LIMIT_BYTES = 52 * 1024 * 1024

F32 = jnp.float32
BF16 = jnp.bfloat16


def _params(*sem):
    return pltpu.CompilerParams(dimension_semantics=sem, vmem_limit_bytes=VMEM_LIMIT_BYTES)


def _tile(n, want):
    t = min(n, want)
    while n % t:
        t //= 2
    return t


def _rms_kernel(x_ref, g_ref, o_ref):
    x = x_ref[...]
    inv = lax.rsqrt(jnp.mean(x * x, axis=-1, keepdims=True) + NORM_EPS)
    o_ref[...] = (x * inv * g_ref[...]).astype(o_ref.dtype)


def _rmsnorm(x, g, out_dtype):
    n, d = x.shape
    tm = _tile(n, 256)
    return pl.pallas_call(
        _rms_kernel,
        grid=(n // tm,),
        in_specs=[pl.BlockSpec((tm, d), lambda i: (i, 0)),
                  pl.BlockSpec((1, d), lambda i: (0, 0))],
        out_specs=pl.BlockSpec((tm, d), lambda i: (i, 0)),
        out_shape=jax.ShapeDtypeStruct((n, d), out_dtype),
        compiler_params=_params("arbitrary"),
    )(x, g.reshape(1, d))


def _mm_kernel(x_ref, w_ref, o_ref):
    o_ref[...] = jnp.dot(x_ref[...], w_ref[...], preferred_element_type=F32).astype(o_ref.dtype)


def _matmul(x, w, out_dtype, tm=512, tn=1024):
    m, k = x.shape
    n = w.shape[1]
    tm, tn = _tile(m, tm), _tile(n, tn)
    return pl.pallas_call(
        _mm_kernel,
        grid=(n // tn, m // tm),
        in_specs=[pl.BlockSpec((tm, k), lambda j, i: (i, 0)),
                  pl.BlockSpec((k, tn), lambda j, i: (0, j))],
        out_specs=pl.BlockSpec((tm, tn), lambda j, i: (i, j)),
        out_shape=jax.ShapeDtypeStruct((m, n), out_dtype),
        compiler_params=_params("arbitrary", "arbitrary"),
    )(x, w)


def _outproj_kernel(a_ref, r_ref, wt_ref, wb_ref, x_ref, o_ref):
    acc = jnp.dot(a_ref[...], wt_ref[...], preferred_element_type=F32)
    acc = acc + jnp.dot(r_ref[...], wb_ref[...], preferred_element_type=F32)
    o_ref[...] = x_ref[...] + acc


def _outproj(attn, rec, w, x, tm=512, tn=1024):
    m, ka = attn.shape
    kr = rec.shape[1]
    assert ka == kr
    n = w.shape[1]
    tm, tn = _tile(m, tm), _tile(n, tn)
    return pl.pallas_call(
        _outproj_kernel,
        grid=(n // tn, m // tm),
        in_specs=[pl.BlockSpec((tm, ka), lambda j, i: (i, 0)),
                  pl.BlockSpec((tm, kr), lambda j, i: (i, 0)),
                  pl.BlockSpec((ka, tn), lambda j, i: (0, j)),
                  pl.BlockSpec((kr, tn), lambda j, i: (1, j)),
                  pl.BlockSpec((tm, tn), lambda j, i: (i, j))],
        out_specs=pl.BlockSpec((tm, tn), lambda j, i: (i, j)),
        out_shape=jax.ShapeDtypeStruct((m, n), F32),
        compiler_params=_params("arbitrary", "arbitrary"),
    )(attn, rec, w, w, x)


def _attn_kernel(slopes_ref, q_ref, k_ref, v_ref, lq1_ref, lk1_ref, lq2_ref, lk2_ref, g_ref,
                 o_ref, m_sc, l_sc, acc_sc, *, tq, tk, seq, lam_init):
    h = pl.program_id(1)
    qi = pl.program_id(2)
    slope = slopes_ref[h]
    q0 = qi * tq

    q = q_ref[...] * jnp.asarray(QK_HALF_DIM ** -0.5, BF16)
    lane = lax.broadcasted_iota(jnp.int32, (tq, V_HEAD_DIM), 1)
    zero = jnp.zeros_like(q)
    qm = jnp.concatenate([jnp.where(lane < QK_HALF_DIM, q, zero),
                          jnp.where(lane >= QK_HALF_DIM, q, zero)], axis=0)

    m_sc[...] = jnp.full(m_sc.shape, -jnp.inf, F32)
    l_sc[...] = jnp.zeros(l_sc.shape, F32)
    acc_sc[...] = jnp.zeros(acc_sc.shape, F32)
    d0 = (lax.broadcasted_iota(jnp.int32, (tq, tk), 0)
          - lax.broadcasted_iota(jnp.int32, (tq, tk), 1))

    def body(kj, carry):
        k0 = pl.multiple_of(kj * tk, tk)
        kb = k_ref[pl.ds(k0, tk), :]
        vb = v_ref[pl.ds(k0, tk), :]
        s = lax.dot_general(qm, kb, (((1,), (1,)), ((), ())), preferred_element_type=F32)
        bias = slope * jnp.abs(d0 + (q0 - k0)).astype(F32)
        s = (s.reshape(2, tq, tk) - bias[None]).reshape(2 * tq, tk)
        m_prev = m_sc[...]
        m_new = jnp.maximum(m_prev, jnp.max(s, axis=1, keepdims=True))
        alpha = jnp.exp(m_prev - m_new)
        p = jnp.exp(s - m_new)
        l_sc[...] = alpha * l_sc[...] + jnp.sum(p, axis=1, keepdims=True)
        acc_sc[...] = alpha * acc_sc[...] + jnp.dot(p.astype(BF16), vb, preferred_element_type=F32)
        m_sc[...] = m_new
        return carry

    lax.fori_loop(0, seq // tk, body, 0)

    lam = (jnp.exp(jnp.sum(lq1_ref[...] * lk1_ref[...], axis=1, keepdims=True))
           - jnp.exp(jnp.sum(lq2_ref[...] * lk2_ref[...], axis=1, keepdims=True))
           + lam_init)
    o = acc_sc[...] / l_sc[...]
    o = o[:tq] - lam * o[tq:]
    inv = lax.rsqrt(jnp.mean(o * o, axis=-1, keepdims=True) + NORM_EPS)
    o_ref[...] = ((o * inv * g_ref[...]) * (1.0 - lam_init)).astype(o_ref.dtype)


def _diff_attention(qkv, batch, seq, n_heads, slopes, lq1, lk1, lq2, lk2, subln_g, lam_init):
    n = batch * seq
    tq = _tile(seq, 256)
    tk = _tile(seq, 512)
    nq = seq // tq
    kernel = functools.partial(_attn_kernel, tq=tq, tk=tk, seq=seq, lam_init=lam_init)
    vec = lambda v: v.reshape(1, -1).astype(F32)
    small = lambda w: pl.BlockSpec((1, w), lambda b, h, i, s: (0, 0))
    grid_spec = pltpu.PrefetchScalarGridSpec(
        num_scalar_prefetch=1,
        grid=(batch, n_heads, nq),
        in_specs=[pl.BlockSpec((tq, V_HEAD_DIM), lambda b, h, i, s: (b * nq + i, h)),
                  pl.BlockSpec((seq, V_HEAD_DIM), lambda b, h, i, s: (b, n_heads + h)),
                  pl.BlockSpec((seq, V_HEAD_DIM), lambda b, h, i, s: (b, 2 * n_heads + h)),
                  small(QK_HALF_DIM), small(QK_HALF_DIM), small(QK_HALF_DIM), small(QK_HALF_DIM),
                  small(V_HEAD_DIM)],
        out_specs=pl.BlockSpec((tq, V_HEAD_DIM), lambda b, h, i, s: (b * nq + i, h)),
        scratch_shapes=[pltpu.VMEM((2 * tq, 1), F32), pltpu.VMEM((2 * tq, 1), F32),
                        pltpu.VMEM((2 * tq, V_HEAD_DIM), F32)],
    )
    return pl.pallas_call(
        kernel,
        grid_spec=grid_spec,
        out_shape=jax.ShapeDtypeStruct((n, n_heads * V_HEAD_DIM), BF16),
        compiler_params=_params("arbitrary", "arbitrary", "arbitrary"),
    )(slopes, qkv, qkv, qkv, vec(lq1), vec(lk1), vec(lq2), vec(lk2), vec(subln_g))


def _gates_kernel(cur_ref, prev_ref, next_ref, cw_ref, cb_ref, wg_ref, bg_ref, lam_ref,
                  af_ref, uf_ref, ab_ref, ub_ref, ext_sc, *, t, n_blocks, blk):
    j = pl.program_id(1)
    nt = pl.num_programs(1)
    halo = SUBLANES
    ext_sc[0:halo, :] = jnp.where(j > 0, prev_ref[...], 0.0)
    ext_sc[halo:halo + t, :] = cur_ref[...]
    ext_sc[halo + t:2 * halo + t, :] = jnp.where(j < nt - 1, next_ref[...], 0.0)
    left = CONV_WIDTH // 2
    y = cb_ref[...]
    for c in range(CONV_WIDTH):
        y = y + ext_sc[halo - left + c:halo - left + c + t, :] * cw_ref[c:c + 1, :]
    nl = -lam_ref[...]
    sp = jnp.maximum(nl, 0.0) + jnp.log1p(jnp.exp(-jnp.abs(nl)))
    outs = ((af_ref, uf_ref), (ab_ref, ub_ref))
    for g in range(n_blocks):
        cs = slice(g * blk, (g + 1) * blk)
        xc = y[:, cs]
        pre = jnp.dot(xc.astype(BF16), wg_ref[g], preferred_element_type=F32) + bg_ref[g:g + 1, :]
        for d in range(2):
            r = jax.nn.sigmoid(pre[:, d * blk:(d + 1) * blk])
            i = jax.nn.sigmoid(pre[:, (2 + d) * blk:(3 + d) * blk])
            log_a = (-RG_C * r) * sp[d:d + 1, cs]
            a = jnp.exp(log_a)
            u = jnp.sqrt(1.0 - jnp.exp(2.0 * log_a)) * (i * xc)
            outs[d][0][:, cs] = a
            outs[d][1][:, cs] = u


def _rg_gates(xg, batch, seq, conv_w, conv_b, wg, bg, lam):
    n = batch * seq
    c = xg.shape[1] // 2
    n_blocks, blk = wg.shape[0], wg.shape[1]
    t = _tile(seq, 256)
    nt = seq // t
    hb = t // SUBLANES
    last = n // SUBLANES - 1
    kernel = functools.partial(_gates_kernel, t=t, n_blocks=n_blocks, blk=blk)
    full = lambda shape: pl.BlockSpec(shape, lambda b, j: (0,) * len(shape))
    out_spec = pl.BlockSpec((t, c), lambda b, j: (b * nt + j, 0))
    return pl.pallas_call(
        kernel,
        grid=(batch, nt),
        in_specs=[pl.BlockSpec((t, c), lambda b, j: (b * nt + j, 0)),
                  pl.BlockSpec((SUBLANES, c), lambda b, j: (jnp.maximum((b * nt + j) * hb - 1, 0), 0)),
                  pl.BlockSpec((SUBLANES, c), lambda b, j: (jnp.minimum((b * nt + j + 1) * hb, last), 0)),
                  full((CONV_WIDTH, c)), full((1, c)),
                  full(wg.shape), full(bg.shape), full((2, c))],
        out_specs=[out_spec] * 4,
        out_shape=[jax.ShapeDtypeStruct((n, c), F32)] * 4,
        scratch_shapes=[pltpu.VMEM((t + 2 * SUBLANES, c), F32)],
        compiler_params=_params("arbitrary", "arbitrary"),
    )(xg, xg, xg, conv_w, conv_b.reshape(1, c), wg, bg, lam)


def _scan_fwd_kernel(a_ref, u_ref, h_ref, carry_sc, *, t):
    @pl.when(pl.program_id(1) == 0)
    def _():
        carry_sc[...] = jnp.zeros(carry_sc.shape, F32)

    def body(i, h):
        h = a_ref[pl.ds(i, 1), :] * h + u_ref[pl.ds(i, 1), :]
        h_ref[pl.ds(i, 1), :] = h
        return h

    carry_sc[...] = lax.fori_loop(0, t, body, carry_sc[...], unroll=8)


def _scan_fwd(a, u, batch, seq):
    n, c = a.shape
    t = _tile(seq, 256)
    nt = seq // t
    spec = pl.BlockSpec((t, c), lambda b, j: (b * nt + j, 0))
    return pl.pallas_call(
        functools.partial(_scan_fwd_kernel, t=t),
        grid=(batch, nt),
        in_specs=[spec, spec],
        out_specs=spec,
        out_shape=jax.ShapeDtypeStruct((n, c), F32),
        scratch_shapes=[pltpu.VMEM((1, c), F32)],
        compiler_params=_params("arbitrary", "arbitrary"),
    )(a, u)


def _scan_bwd_kernel(a_ref, u_ref, hf_ref, gr_ref, g_ref, o_ref, carry_sc, hb_sc, *, t):
    @pl.when(pl.program_id(1) == 0)
    def _():
        carry_sc[...] = jnp.zeros(carry_sc.shape, F32)

    def body(i, h):
        r = t - 1 - i
        h = a_ref[pl.ds(r, 1), :] * h + u_ref[pl.ds(r, 1), :]
        hb_sc[pl.ds(r, 1), :] = h
        return h

    carry_sc[...] = lax.fori_loop(0, t, body, carry_sc[...], unroll=8)
    rec = (hf_ref[...] + hb_sc[...]) * jax.nn.gelu(gr_ref[...], approximate=True)
    inv = lax.rsqrt(jnp.mean(rec * rec, axis=-1, keepdims=True) + NORM_EPS)
    o_ref[...] = (rec * inv * g_ref[...]).astype(o_ref.dtype)


def _scan_bwd_finish(a, u, hf, xg, norm_g, batch, seq):
    n, c = a.shape
    t = _tile(seq, 256)
    nt = seq // t
    rev = lambda b, j: (b * nt + (nt - 1 - j), 0)
    spec = pl.BlockSpec((t, c), rev)
    return pl.pallas_call(
        functools.partial(_scan_bwd_kernel, t=t),
        grid=(batch, nt),
        in_specs=[spec, spec, spec,
                  pl.BlockSpec((t, c), lambda b, j: (b * nt + (nt - 1 - j), 1)),
                  pl.BlockSpec((1, c), lambda b, j: (0, 0))],
        out_specs=spec,
        out_shape=jax.ShapeDtypeStruct((n, c), BF16),
        scratch_shapes=[pltpu.VMEM((1, c), F32), pltpu.VMEM((t, c), F32)],
        compiler_params=_params("arbitrary", "arbitrary"),
    )(a, u, hf, xg, norm_g.reshape(1, c))


def _router_kernel(h_ref, g_ref, rw_ref, hn_ref, aff_ref, *, n_experts):
    x = h_ref[...]
    inv = lax.rsqrt(jnp.mean(x * x, axis=-1, keepdims=True) + NORM_EPS)
    hn = x * inv * g_ref[...]
    hn_ref[...] = hn
    logits = jnp.dot(hn.astype(BF16), rw_ref[...], preferred_element_type=F32)
    lane = lax.broadcasted_iota(jnp.int32, logits.shape, 1)
    logits = jnp.where(lane < n_experts, logits, -jnp.inf)
    e = jnp.exp(logits - jnp.max(logits, axis=-1, keepdims=True))
    aff_ref[...] = e / jnp.sum(e, axis=-1, keepdims=True)


def _router(h, g, rw_pad, n_experts):
    n, d = h.shape
    tm = _tile(n, 256)
    return pl.pallas_call(
        functools.partial(_router_kernel, n_experts=n_experts),
        grid=(n // tm,),
        in_specs=[pl.BlockSpec((tm, d), lambda i: (i, 0)),
                  pl.BlockSpec((1, d), lambda i: (0, 0)),
                  pl.BlockSpec((d, LANES), lambda i: (0, 0))],
        out_specs=[pl.BlockSpec((tm, d), lambda i: (i, 0)),
                   pl.BlockSpec((tm, LANES), lambda i: (i, 0))],
        out_shape=[jax.ShapeDtypeStruct((n, d), F32), jax.ShapeDtypeStruct((n, LANES), F32)],
        compiler_params=_params("arbitrary"),
    )(h, g.reshape(1, d), rw_pad)


def _threshold_kernel(bits_ref, thr_ref, need_ref, *, cap):
    bits = bits_ref[...]

    def count(mask):
        return jnp.sum(mask.astype(F32), axis=1, keepdims=True)

    def body(i, thr):
        cand = thr | lax.shift_left(jnp.int32(1), 30 - i)
        return jnp.where(count(bits >= cand) >= cap, cand, thr)

    thr = lax.fori_loop(0, 31, body, jnp.zeros((bits.shape[0], 1), jnp.int32))
    need = cap - count(bits > thr).astype(jnp.int32)
    thr_ref[...] = jnp.broadcast_to(thr, thr_ref.shape)
    need_ref[...] = jnp.broadcast_to(need, need_ref.shape)


def _thresholds(bits, cap):
    e, n = bits.shape
    out = jax.ShapeDtypeStruct((e, LANES), jnp.int32)
    return pl.pallas_call(
        functools.partial(_threshold_kernel, cap=cap),
        out_shape=[out, out],
        compiler_params=pltpu.CompilerParams(vmem_limit_bytes=VMEM_LIMIT_BYTES),
    )(bits)


def _compact_kernel(thr_ref, need_ref, bits_ref, idx_ref, gate_ref, *, n_tok):
    e = pl.program_id(0)
    thr = thr_ref[e]
    need = need_ref[e]

    def body(n, carry):
        c, ties = carry
        v = bits_ref[n]
        is_tie = v == thr
        take = jnp.logical_or(v > thr, jnp.logical_and(is_tie, ties < need))

        @pl.when(take)
        def _():
            idx_ref[c] = n
            gate_ref[c] = v

        return c + take.astype(jnp.int32), ties + is_tie.astype(jnp.int32)

    lax.fori_loop(0, n_tok, body, (jnp.int32(0), jnp.int32(0)))


def _compact(bits_flat, thr, need, n_experts, n_tok, cap):
    grid_spec = pltpu.PrefetchScalarGridSpec(
        num_scalar_prefetch=2,
        grid=(n_experts,),
        in_specs=[pl.BlockSpec((n_tok,), lambda e, t, d: (e,), memory_space=pltpu.SMEM)],
        out_specs=[pl.BlockSpec((cap,), lambda e, t, d: (e,), memory_space=pltpu.SMEM)] * 2,
    )
    out = jax.ShapeDtypeStruct((n_experts * cap,), jnp.int32)
    return pl.pallas_call(
        functools.partial(_compact_kernel, n_tok=n_tok),
        grid_spec=grid_spec,
        out_shape=[out, out],
        compiler_params=_params("arbitrary"),
    )(thr, need, bits_flat)


def _row_copy(src, dst, src_row, dst_row, sem):
    return pltpu.make_async_copy(src.at[pl.ds(src_row, 1), :], dst.at[pl.ds(dst_row, 1), :], sem)


def _gather_kernel(idx_ref, src_ref, o_ref, buf, sem, *, tg):
    base = pl.program_id(0) * tg

    def start(r, c):
        _row_copy(src_ref, buf, idx_ref[base + r], r, sem).start()
        return c

    def wait(r, c):
        _row_copy(src_ref, buf, 0, r, sem).wait()
        return c

    lax.fori_loop(0, tg, start, 0)
    lax.fori_loop(0, tg, wait, 0)
    o_ref[...] = buf[...].astype(o_ref.dtype)


def _gather_rows(src, idx, out_dtype):
    n_slots = idx.shape[0]
    d = src.shape[1]
    tg = _tile(n_slots, 256)
    grid_spec = pltpu.PrefetchScalarGridSpec(
        num_scalar_prefetch=1,
        grid=(n_slots // tg,),
        in_specs=[pl.BlockSpec(memory_space=pl.ANY)],
        out_specs=pl.BlockSpec((tg, d), lambda i, s: (i, 0)),
        scratch_shapes=[pltpu.VMEM((tg, d), src.dtype), pltpu.SemaphoreType.DMA],
    )
    return pl.pallas_call(
        functools.partial(_gather_kernel, tg=tg),
        grid_spec=grid_spec,
        out_shape=jax.ShapeDtypeStruct((n_slots, d), out_dtype),
        compiler_params=_params("arbitrary"),
    )(idx, src)


def _scatter_add_kernel(idx_ref, ye_ref, h_in_ref, h_ref, buf, sem, *, tg):
    del h_in_ref
    base = pl.program_id(0) * tg

    def start_in(r, c):
        _row_copy(h_ref, buf, idx_ref[base + r], r, sem).start()
        return c

    def wait_in(r, c):
        _row_copy(h_ref, buf, 0, r, sem).wait()
        return c

    def start_out(r, c):
        _row_copy(buf, h_ref, r, idx_ref[base + r], sem).start()
        return c

    def wait_out(r, c):
        _row_copy(buf, h_ref, r, 0, sem).wait()
        return c

    lax.fori_loop(0, tg, start_in, 0)
    lax.fori_loop(0, tg, wait_in, 0)
    buf[...] = buf[...] + ye_ref[...]
    lax.fori_loop(0, tg, start_out, 0)
    lax.fori_loop(0, tg, wait_out, 0)


def _scatter_add(h, ye, idx, cap):
    n_slots, d = ye.shape
    tg = _tile(cap, 256)
    grid_spec = pltpu.PrefetchScalarGridSpec(
        num_scalar_prefetch=1,
        grid=(n_slots // tg,),
        in_specs=[pl.BlockSpec((tg, d), lambda i, s: (i, 0)),
                  pl.BlockSpec(memory_space=pl.ANY)],
        out_specs=pl.BlockSpec(memory_space=pl.ANY),
        scratch_shapes=[pltpu.VMEM((tg, d), F32), pltpu.SemaphoreType.DMA],
    )
    return pl.pallas_call(
        functools.partial(_scatter_add_kernel, tg=tg),
        grid_spec=grid_spec,
        out_shape=jax.ShapeDtypeStruct(h.shape, h.dtype),
        input_output_aliases={2: 0},
        compiler_params=_params("arbitrary"),
    )(idx, ye, h)


def _ffn_up_kernel(x_ref, w1_ref, w3_ref, o_ref):
    x = x_ref[...]
    a = jnp.dot(x, w1_ref[...], preferred_element_type=F32)
    b = jnp.dot(x, w3_ref[...], preferred_element_type=F32)
    o_ref[...] = (jax.nn.silu(a) * b).astype(o_ref.dtype)


def _ffn_up(xe, w1, w3, cap, tm=512, tn=512):
    n_experts, d, f = w1.shape
    tm, tn = _tile(cap, tm), _tile(f, tn)
    nm = cap // tm
    wspec = pl.BlockSpec((None, d, tn), lambda e, j, i: (e, 0, j))
    return pl.pallas_call(
        _ffn_up_kernel,
        grid=(n_experts, f // tn, nm),
        in_specs=[pl.BlockSpec((tm, d), lambda e, j, i: (e * nm + i, 0)), wspec, wspec],
        out_specs=pl.BlockSpec((tm, tn), lambda e, j, i: (e * nm + i, j)),
        out_shape=jax.ShapeDtypeStruct((n_experts * cap, f), BF16),
        compiler_params=_params("arbitrary", "arbitrary", "arbitrary"),
    )(xe, w1, w3)


def _ffn_down_kernel(h_ref, w2_ref, gate_ref, o_ref):
    o_ref[...] = jnp.dot(h_ref[...], w2_ref[...], preferred_element_type=F32) * gate_ref[...]


def _ffn_down(hm, w2, gates, cap, tm=512, tn=512):
    n_experts, f, d = w2.shape
    tm, tn = _tile(cap, tm), _tile(d, tn)
    nm = cap // tm
    return pl.pallas_call(
        _ffn_down_kernel,
        grid=(n_experts, d // tn, nm),
        in_specs=[pl.BlockSpec((tm, f), lambda e, j, i: (e * nm + i, 0)),
                  pl.BlockSpec((None, f, tn), lambda e, j, i: (e, 0, j)),
                  pl.BlockSpec((tm, 1), lambda e, j, i: (e * nm + i, 0))],
        out_specs=pl.BlockSpec((tm, tn), lambda e, j, i: (e * nm + i, j)),
        out_shape=jax.ShapeDtypeStruct((n_experts * cap, d), F32),
        compiler_params=_params("arbitrary", "arbitrary", "arbitrary"),
    )(hm, w2, gates)


def _prepare_weights(w_in, rg_wa, rg_ba, rg_wx, rg_bx, w_out, router_w, exp_w1, exp_w3, exp_w2,
                     attn_width):
    n_blocks, blk = rg_wa.shape[1], rg_wa.shape[2]
    wg = jnp.concatenate([rg_wa[0], rg_wa[1], rg_wx[0], rg_wx[1]], axis=-1).astype(BF16)
    bg = jnp.concatenate([rg_ba[0].reshape(n_blocks, blk), rg_ba[1].reshape(n_blocks, blk),
                          rg_bx[0].reshape(n_blocks, blk), rg_bx[1].reshape(n_blocks, blk)], axis=-1)
    n_experts = router_w.shape[1]
    rw_pad = jnp.pad(router_w, ((0, 0), (0, LANES - n_experts))).astype(BF16)
    return dict(
        w_qkv=w_in[:, :3 * attn_width].astype(BF16),
        w_rec=w_in[:, 3 * attn_width:].astype(BF16),
        wg=wg, bg=bg, w_out=w_out.astype(BF16), rw_pad=rw_pad,
        w1=exp_w1.astype(BF16), w3=exp_w3.astype(BF16), w2=exp_w2.astype(BF16))


def _layer(x, layer, pw, norm_mix_g, lambda_q1, lambda_k1, lambda_q2, lambda_k2, attn_subln_g,
           conv_w, conv_b, rg_lambda, rec_norm_g, norm_ffn_g, n_experts, attn_width):
    batch, seq, d = x.shape
    n = batch * seq
    n_heads = attn_width // V_HEAD_DIM
    x2 = x.reshape(n, d)

    xn = _rmsnorm(x2, norm_mix_g, BF16)
    qkv = _matmul(xn, pw["w_qkv"], BF16)
    xg = _matmul(xn, pw["w_rec"], F32)
    lam_init = 0.8 - 0.6 * math.exp(-0.3 * layer)
    slopes = 2.0 ** (-8.0 * jnp.arange(1, n_heads + 1, dtype=F32) / n_heads)
    attn = _diff_attention(qkv, batch, seq, n_heads, slopes, lambda_q1, lambda_k1, lambda_q2,
                           lambda_k2, attn_subln_g, lam_init)
    a_f, u_f, a_b, u_b = _rg_gates(xg, batch, seq, conv_w, conv_b, pw["wg"], pw["bg"], rg_lambda)
    h_f = _scan_fwd(a_f, u_f, batch, seq)
    rec = _scan_bwd_finish(a_b, u_b, h_f, xg, rec_norm_g, batch, seq)
    h = _outproj(attn, rec, pw["w_out"], x2)

    cap = (EC_CAPACITY_FACTOR * n) // n_experts
    hn, aff = _router(h, norm_ffn_g, pw["rw_pad"], n_experts)
    bits = lax.bitcast_convert_type(aff[:, :n_experts].T, jnp.int32)
    thr, need = _thresholds(bits, cap)
    idx, gate_bits = _compact(bits.reshape(-1), thr[:, 0], need[:, 0], n_experts, n, cap)
    gates = lax.bitcast_convert_type(gate_bits, F32).reshape(-1, 1)
    xe = _gather_rows(hn, idx, BF16)
    hm = _ffn_up(xe, pw["w1"], pw["w3"], cap)
    ye = _ffn_down(hm, pw["w2"], gates, cap)
    return _scatter_add(h, ye, idx, cap)


def _trunk(x, pw, norm_mix_g, lambda_q1, lambda_k1, lambda_q2, lambda_k2, attn_subln_g, conv_w,
           conv_b, rg_lambda, rec_norm_g, norm_ffn_g, final_norm_g, n_experts, attn_width):
    batch, seq, d = x.shape
    h = x
    for l in range(norm_mix_g.shape[0]):
        h = _layer(h.reshape(batch, seq, d), l, pw[l], norm_mix_g[l], lambda_q1[l], lambda_k1[l],
                   lambda_q2[l], lambda_k2[l], attn_subln_g[l], conv_w[l], conv_b[l], rg_lambda[l],
                   rec_norm_g[l], norm_ffn_g[l], n_experts, attn_width)
    return _rmsnorm(h.reshape(batch * seq, d), final_norm_g, F32).reshape(batch, seq, d)


def kernel(x_prompt, x_sample, norm_mix_g, w_in, lambda_q1, lambda_k1, lambda_q2, lambda_k2, attn_subln_g, conv_w, conv_b, rg_wa, rg_ba, rg_wx, rg_bx, rg_lambda, rec_norm_g, w_out, norm_ffn_g, router_w, exp_w1, exp_w3, exp_w2, final_norm_g):
    depth = w_in.shape[0]
    rec_width = conv_w.shape[-1]
    attn_width = (w_in.shape[-1] - 2 * rec_width) // 3
    n_experts = router_w.shape[-1]
    pw = [_prepare_weights(w_in[l], rg_wa[l], rg_ba[l], rg_wx[l], rg_bx[l], w_out[l], router_w[l],
                           exp_w1[l], exp_w3[l], exp_w2[l], attn_width) for l in range(depth)]
    run = lambda x: _trunk(x, pw, norm_mix_g, lambda_q1, lambda_k1, lambda_q2, lambda_k2,
                           attn_subln_g, conv_w, conv_b, rg_lambda, rec_norm_g, norm_ffn_g,
                           final_norm_g, n_experts, attn_width)
    return (run(x_prompt), run(x_sample))
```

```python
import functools
import math

import jax
import jax.numpy as jnp
from jax import lax
from jax.experimental import pallas as pl
from jax.experimental.pallas import tpu as pltpu

V_HEAD_DIM = 128
QK_HALF_DIM = V_HEAD_DIM // 2
CONV_WIDTH = 4
RG_C = 8.0
EC_CAPACITY_FACTOR = 2
NORM_EPS = 1e-6
LANES = 128
SUBLANES = 8
SMEM_TILE_1D = 1024
VMEM_LIMIT_BYTES = 52 * 1024 * 1024

ATTN_TQ = 256
ATTN_TK = 1024
LOG2E = math.log2(math.e)
SKIP_LOG2 = 100.0
FAST_LOG2 = 110.0
NORM_SLACK = 1.01

F32 = jnp.float32
BF16 = jnp.bfloat16


def _params(*sem):
    return pltpu.CompilerParams(dimension_semantics=sem, vmem_limit_bytes=VMEM_LIMIT_BYTES)


def _tile(n, want):
    t = min(n, want)
    while n % t:
        t //= 2
    return t


def _rms_kernel(x_ref, g_ref, o_ref):
    x = x_ref[...]
    inv = lax.rsqrt(jnp.mean(x * x, axis=-1, keepdims=True) + NORM_EPS)
    o_ref[...] = (x * inv * g_ref[...]).astype(o_ref.dtype)


def _rmsnorm(x, g, out_dtype):
    n, d = x.shape
    tm = _tile(n, 256)
    return pl.pallas_call(
        _rms_kernel,
        grid=(n // tm,),
        in_specs=[pl.BlockSpec((tm, d), lambda i: (i, 0)),
                  pl.BlockSpec((1, d), lambda i: (0, 0))],
        out_specs=pl.BlockSpec((tm, d), lambda i: (i, 0)),
        out_shape=jax.ShapeDtypeStruct((n, d), out_dtype),
        compiler_params=_params("arbitrary"),
        name="rmsnorm",
    )(x, g.reshape(1, d))


def _mm_kernel(x_ref, w_ref, s_ref, o_ref, wb_sc):
    @pl.when(pl.program_id(1) == 0)
    def _():
        wb_sc[...] = w_ref[...].astype(BF16)

    acc = jnp.dot(x_ref[...], wb_sc[...], preferred_element_type=F32)
    o_ref[...] = (acc * s_ref[...]).astype(o_ref.dtype)


def _matmul(x, w, col0, n, col_scale, out_dtype, tm=512, tn=512):
    m, k = x.shape
    tm, tn = _tile(m, tm), _tile(math.gcd(n, col0), tn)
    jb = col0 // tn
    return pl.pallas_call(
        _mm_kernel,
        grid=(n // tn, m // tm),
        in_specs=[pl.BlockSpec((tm, k), lambda j, i: (i, 0)),
                  pl.BlockSpec((k, tn), lambda j, i: (0, jb + j)),
                  pl.BlockSpec((1, tn), lambda j, i: (0, j))],
        out_specs=pl.BlockSpec((tm, tn), lambda j, i: (i, j)),
        out_shape=jax.ShapeDtypeStruct((m, n), out_dtype),
        scratch_shapes=[pltpu.VMEM((k, tn), BF16)],
        compiler_params=_params("arbitrary", "arbitrary"),
        name="in_proj",
    )(x, w, col_scale)


def _outproj_kernel(a_ref, r_ref, wt_ref, wb_ref, x_ref, o_ref, wt_sc, wb_sc):
    @pl.when(pl.program_id(1) == 0)
    def _():
        wt_sc[...] = wt_ref[...].astype(BF16)
        wb_sc[...] = wb_ref[...].astype(BF16)

    acc = jnp.dot(a_ref[...], wt_sc[...], preferred_element_type=F32)
    acc = acc + jnp.dot(r_ref[...], wb_sc[...], preferred_element_type=F32)
    o_ref[...] = x_ref[...] + acc


def _outproj(attn, rec, w, x, tm=512, tn=512):
    m, ka = attn.shape
    kr = rec.shape[1]
    assert ka == kr
    n = w.shape[1]
    tm, tn = _tile(m, tm), _tile(n, tn)
    return pl.pallas_call(
        _outproj_kernel,
        grid=(n // tn, m // tm),
        in_specs=[pl.BlockSpec((tm, ka), lambda j, i: (i, 0)),
                  pl.BlockSpec((tm, kr), lambda j, i: (i, 0)),
                  pl.BlockSpec((ka, tn), lambda j, i: (0, j)),
                  pl.BlockSpec((kr, tn), lambda j, i: (1, j)),
                  pl.BlockSpec((tm, tn), lambda j, i: (i, j))],
        out_specs=pl.BlockSpec((tm, tn), lambda j, i: (i, j)),
        out_shape=jax.ShapeDtypeStruct((m, n), F32),
        scratch_shapes=[pltpu.VMEM((ka, tn), BF16), pltpu.VMEM((kr, tn), BF16)],
        compiler_params=_params("arbitrary", "arbitrary"),
        name="out_proj",
    )(attn, rec, w, w, x)


def _qk_bound_kernel(q_ref, k_ref, g_ref, o_ref):
    @pl.when(pl.program_id(1) == 0)
    def _():
        o_ref[...] = jnp.zeros(o_ref.shape, F32)

    def group_max(x_ref):
        x = x_ref[...].astype(F32)
        sq = x * x
        hi = sq.astype(BF16)
        lo = (sq - hi.astype(F32)).astype(BF16)
        s = (jnp.dot(hi, g_ref[...], preferred_element_type=F32)
             + jnp.dot(lo, g_ref[...], preferred_element_type=F32))
        return jnp.max(s, axis=0, keepdims=True)

    o_ref[0:1, :] = jnp.maximum(o_ref[0:1, :], group_max(q_ref))
    o_ref[1:2, :] = jnp.maximum(o_ref[1:2, :], group_max(k_ref))


def _qk_bounds(qkv, batch, seq, attn_width):
    n_groups = attn_width // QK_HALF_DIM
    assert n_groups <= LANES
    tm = _tile(seq, 512)
    nt = seq // tm
    g = (jnp.arange(attn_width)[:, None] // QK_HALF_DIM == jnp.arange(LANES)[None, :]).astype(BF16)
    return pl.pallas_call(
        _qk_bound_kernel,
        grid=(batch, nt),
        in_specs=[pl.BlockSpec((tm, attn_width), lambda b, j: (b * nt + j, 0)),
                  pl.BlockSpec((tm, attn_width), lambda b, j: (b * nt + j, 1)),
                  pl.BlockSpec((attn_width, LANES), lambda b, j: (0, 0))],
        out_specs=pl.BlockSpec((None, 2, LANES), lambda b, j: (b, 0, 0)),
        out_shape=jax.ShapeDtypeStruct((batch, 2, LANES), F32),
        compiler_params=_params("arbitrary", "arbitrary"),
        name="qk_bounds",
    )(qkv, qkv, g)


def _attn_kernel(slopes_ref, band_ref, fast_ref, kmax_ref, q_ref, k_ref, v_ref, lq1_ref, lk1_ref,
                 lq2_ref, lk2_ref, g_ref, o_ref, m_sc, l_sc, acc_sc, lp_sc,
                 *, tq, tk, seq, n_heads, lam_init):
    b = pl.program_id(0)
    h = pl.program_id(1)
    bh = b * n_heads + h
    slope = slopes_ref[h]
    band = band_ref[bh]
    nk = seq // tk
    nch = tk // LANES
    q0 = pl.program_id(2) * tq
    kd = lax.div(q0, tk)
    lo = jnp.maximum(kd - band, 0)
    hi = jnp.minimum(kd + 1 + band, nk)

    q = q_ref[...]
    lane = lax.broadcasted_iota(jnp.int32, (tq, V_HEAD_DIM), 1)
    zero = jnp.zeros_like(q)
    qm = jnp.concatenate([jnp.where(lane < QK_HALF_DIM, q, zero),
                          jnp.where(lane >= QK_HALF_DIM, q, zero)], axis=0)
    d0 = (lax.broadcasted_iota(jnp.int32, (tq, tk), 0)
          - lax.broadcasted_iota(jnp.int32, (tq, tk), 1))

    def scores(k0):
        return lax.dot_general(qm, k_ref[pl.ds(k0, tk), :], (((1,), (1,)), ((), ())),
                               preferred_element_type=F32)

    def abs_bias(k0):
        bias = slope * jnp.abs(d0 + (q0 - k0)).astype(F32)
        return jnp.concatenate([bias, bias], axis=0)

    def finalize(l):
        lam = (jnp.exp(jnp.sum(lq1_ref[...] * lk1_ref[...], axis=1, keepdims=True))
               - jnp.exp(jnp.sum(lq2_ref[...] * lk2_ref[...], axis=1, keepdims=True))
               + lam_init)
        o = acc_sc[...] / l
        o = o[:tq] - lam * o[tq:]
        inv = lax.rsqrt(jnp.mean(o * o, axis=-1, keepdims=True) + NORM_EPS)
        o_ref[...] = ((o * inv * g_ref[...]) * (1.0 - lam_init)).astype(o_ref.dtype)

    @pl.when(fast_ref[bh] != 0)
    def _():
        qf = qm.astype(F32)
        nq = jnp.sqrt(jnp.sum(qf * qf, axis=1, keepdims=True))
        row = lax.broadcasted_iota(jnp.int32, (2 * tq, 1), 0)
        km = jnp.where(row < tq, kmax_ref[2 * bh], kmax_ref[2 * bh + 1])
        m_rep = jnp.broadcast_to(nq * km, (2 * tq, LANES))
        il = lax.broadcasted_iota(jnp.int32, (2 * tq, LANES), 0)
        il = slope * jnp.where(il >= tq, il - tq, il).astype(F32)
        jl = lax.broadcasted_iota(jnp.int32, (1, tk), 1).astype(F32)
        col_l = slope * (jl - (tk - 1))
        col_r = -slope * jl
        acc_sc[...] = jnp.zeros(acc_sc.shape, F32)
        lp_sc[...] = jnp.zeros(lp_sc.shape, F32)

        def tiled(x):
            return jnp.concatenate([x] * nch, axis=1)

        def accumulate(e, k0):
            p = jnp.exp2(e)
            lp = lp_sc[...]
            for c in range(nch):
                lp = lp + p[:, c * LANES:(c + 1) * LANES]
            lp_sc[...] = lp
            acc_sc[...] += jnp.dot(p.astype(BF16), v_ref[pl.ds(k0, tk), :],
                                   preferred_element_type=F32)

        k0d = pl.multiple_of(kd * tk, tk)
        accumulate((scores(k0d) - abs_bias(k0d)) - tiled(m_rep), k0d)

        def left(kj, c):
            k0 = pl.multiple_of(kj * tk, tk)
            mrow = (m_rep + il) - slope * (k0 + (tk - 1) - q0).astype(F32)
            accumulate((scores(k0) + col_l) - tiled(mrow), k0)
            return c

        def right(kj, c):
            k0 = pl.multiple_of(kj * tk, tk)
            mrow = (m_rep - il) - slope * (q0 - k0).astype(F32)
            accumulate((scores(k0) + col_r) - tiled(mrow), k0)
            return c

        lax.fori_loop(lo, kd, left, 0)
        lax.fori_loop(kd + 1, hi, right, 0)
        finalize(jnp.sum(lp_sc[...], axis=1, keepdims=True))

    @pl.when(fast_ref[bh] == 0)
    def _():
        m_sc[...] = jnp.full(m_sc.shape, -jnp.inf, F32)
        l_sc[...] = jnp.zeros(l_sc.shape, F32)
        acc_sc[...] = jnp.zeros(acc_sc.shape, F32)

        def body(kj, c):
            k0 = pl.multiple_of(kj * tk, tk)
            s = scores(k0) - abs_bias(k0)
            m_prev = m_sc[...]
            m_new = jnp.maximum(m_prev, jnp.max(s, axis=1, keepdims=True))
            alpha = jnp.exp2(m_prev - m_new)
            p = jnp.exp2(s - m_new)
            l_sc[...] = alpha * l_sc[...] + jnp.sum(p, axis=1, keepdims=True)
            acc_sc[...] = alpha * acc_sc[...] + jnp.dot(p.astype(BF16), v_ref[pl.ds(k0, tk), :],
                                                         preferred_element_type=F32)
            m_sc[...] = m_new
            return c

        lax.fori_loop(lo, hi, body, 0)
        finalize(l_sc[...])


def _diff_attention(qkv, batch, seq, n_heads, slopes, lq1, lk1, lq2, lk2, subln_g, lam_init):
    n = batch * seq
    attn_width = n_heads * V_HEAD_DIM
    tq = _tile(seq, ATTN_TQ)
    tk = _tile(seq, ATTN_TK)
    assert tk % tq == 0 and tk % LANES == 0
    nq = seq // tq

    sq = _qk_bounds(qkv, batch, seq, attn_width)
    norms = jnp.sqrt(sq[:, :, :2 * n_heads]).reshape(batch, 2, n_heads, 2) * NORM_SLACK
    qmax, kmax = norms[:, 0], norms[:, 1]
    bmax = jnp.max(qmax * kmax, axis=-1)
    slopes2 = slopes * LOG2E
    band = jnp.floor((2.0 * bmax + SKIP_LOG2) / (slopes2[None, :] * tk)) + 1.0
    band = jnp.minimum(band, seq // tk).astype(jnp.int32).reshape(-1)
    fast = (2.0 * bmax <= FAST_LOG2).astype(jnp.int32).reshape(-1)

    kernel = functools.partial(_attn_kernel, tq=tq, tk=tk, seq=seq, n_heads=n_heads,
                               lam_init=lam_init)
    vec = lambda v: v.reshape(1, -1).astype(F32)
    small = lambda w: pl.BlockSpec((1, w), lambda b, h, i, *_: (0, 0))
    grid_spec = pltpu.PrefetchScalarGridSpec(
        num_scalar_prefetch=4,
        grid=(batch, n_heads, nq),
        in_specs=[pl.BlockSpec((tq, V_HEAD_DIM), lambda b, h, i, *_: (b * nq + i, h)),
                  pl.BlockSpec((seq, V_HEAD_DIM), lambda b, h, i, *_: (b, n_heads + h)),
                  pl.BlockSpec((seq, V_HEAD_DIM), lambda b, h, i, *_: (b, 2 * n_heads + h)),
                  small(QK_HALF_DIM), small(QK_HALF_DIM), small(QK_HALF_DIM), small(QK_HALF_DIM),
                  small(V_HEAD_DIM)],
        out_specs=pl.BlockSpec((tq, V_HEAD_DIM), lambda b, h, i, *_: (b * nq + i, h)),
        scratch_shapes=[pltpu.VMEM((2 * tq, 1), F32), pltpu.VMEM((2 * tq, 1), F32),
                        pltpu.VMEM((2 * tq, V_HEAD_DIM), F32), pltpu.VMEM((2 * tq, LANES), F32)],
    )
    return pl.pallas_call(
        kernel,
        grid_spec=grid_spec,
        out_shape=jax.ShapeDtypeStruct((n, attn_width), BF16),
        compiler_params=_params("arbitrary", "arbitrary", "arbitrary"),
        name="diff_attn",
    )(slopes2, band, fast, kmax.reshape(-1), qkv, qkv, qkv,
      vec(lq1), vec(lk1), vec(lq2), vec(lk2), vec(subln_g))


def _gates_kernel(cur_ref, prev_ref, next_ref, cw_ref, cb_ref, wg_ref, bg_ref, lam_ref,
                  af_ref, uf_ref, ab_ref, ub_ref, ext_sc, *, t, n_blocks, blk):
    j = pl.program_id(1)
    nt = pl.num_programs(1)
    halo = SUBLANES
    ext_sc[0:halo, :] = jnp.where(j > 0, prev_ref[...], 0.0)
    ext_sc[halo:halo + t, :] = cur_ref[...]
    ext_sc[halo + t:2 * halo + t, :] = jnp.where(j < nt - 1, next_ref[...], 0.0)
    left = CONV_WIDTH // 2
    y = cb_ref[...]
    for c in range(CONV_WIDTH):
        y = y + ext_sc[halo - left + c:halo - left + c + t, :] * cw_ref[c:c + 1, :]
    nl = -lam_ref[...]
    sp = jnp.maximum(nl, 0.0) + jnp.log1p(jnp.exp(-jnp.abs(nl)))
    outs = ((af_ref, uf_ref), (ab_ref, ub_ref))
    for g in range(n_blocks):
        cs = slice(g * blk, (g + 1) * blk)
        xc = y[:, cs]
        pre = jnp.dot(xc.astype(BF16), wg_ref[g], preferred_element_type=F32) + bg_ref[g:g + 1, :]
        for d in range(2):
            r = jax.nn.sigmoid(pre[:, d * blk:(d + 1) * blk])
            i = jax.nn.sigmoid(pre[:, (2 + d) * blk:(3 + d) * blk])
            log_a = (-RG_C * r) * sp[d:d + 1, cs]
            a = jnp.exp(log_a)
            u = jnp.sqrt(1.0 - jnp.exp(2.0 * log_a)) * (i * xc)
            outs[d][0][:, cs] = a
            outs[d][1][:, cs] = u


def _rg_gates(xg, batch, seq, conv_w, conv_b, wg, bg, lam):
    n = batch * seq
    c = xg.shape[1] // 2
    n_blocks, blk = wg.shape[0], wg.shape[1]
    t = _tile(seq, 256)
    nt = seq // t
    hb = t // SUBLANES
    last = n // SUBLANES - 1
    kernel = functools.partial(_gates_kernel, t=t, n_blocks=n_blocks, blk=blk)
    full = lambda shape: pl.BlockSpec(shape, lambda b, j: (0,) * len(shape))
    out_spec = pl.BlockSpec((t, c), lambda b, j: (b * nt + j, 0))
    return pl.pallas_call(
        kernel,
        grid=(batch, nt),
        in_specs=[pl.BlockSpec((t, c), lambda b, j: (b * nt + j, 0)),
                  pl.BlockSpec((SUBLANES, c), lambda b, j: (jnp.maximum((b * nt + j) * hb - 1, 0), 0)),
                  pl.BlockSpec((SUBLANES, c), lambda b, j: (jnp.minimum((b * nt + j + 1) * hb, last), 0)),
                  full((CONV_WIDTH, c)), full((1, c)),
                  full(wg.shape), full(bg.shape), full((2, c))],
        out_specs=[out_spec] * 4,
        out_shape=[jax.ShapeDtypeStruct((n, c), F32)] * 4,
        scratch_shapes=[pltpu.VMEM((t + 2 * SUBLANES, c), F32)],
        compiler_params=_params("arbitrary", "arbitrary"),
        name="rg_gates",
    )(xg, xg, xg, conv_w, conv_b.reshape(1, c), wg, bg, lam)


def _scan_fwd_kernel(a_ref, u_ref, h_ref, carry_sc, *, t):
    @pl.when(pl.program_id(1) == 0)
    def _():
        carry_sc[...] = jnp.zeros(carry_sc.shape, F32)

    def body(i, h):
        h = a_ref[pl.ds(i, 1), :] * h + u_ref[pl.ds(i, 1), :]
        h_ref[pl.ds(i, 1), :] = h
        return h

    carry_sc[...] = lax.fori_loop(0, t, body, carry_sc[...], unroll=8)


def _scan_fwd(a, u, batch, seq):
    n, c = a.shape
    t = _tile(seq, 256)
    nt = seq // t
    spec = pl.BlockSpec((t, c), lambda b, j: (b * nt + j, 0))
    return pl.pallas_call(
        functools.partial(_scan_fwd_kernel, t=t),
        grid=(batch, nt),
        in_specs=[spec, spec],
        out_specs=spec,
        out_shape=jax.ShapeDtypeStruct((n, c), F32),
        scratch_shapes=[pltpu.VMEM((1, c), F32)],
        compiler_params=_params("arbitrary", "arbitrary"),
        name="scan_fwd",
    )(a, u)


def _scan_bwd_kernel(a_ref, u_ref, hf_ref, gr_ref, g_ref, o_ref, carry_sc, hb_sc, *, t):
    @pl.when(pl.program_id(1) == 0)
    def _():
        carry_sc[...] = jnp.zeros(carry_sc.shape, F32)

    def body(i, h):
        r = t - 1 - i
        h = a_ref[pl.ds(r, 1), :] * h + u_ref[pl.ds(r, 1), :]
        hb_sc[pl.ds(r, 1), :] = h
        return h

    carry_sc[...] = lax.fori_loop(0, t, body, carry_sc[...], unroll=8)
    rec = (hf_ref[...] + hb_sc[...]) * jax.nn.gelu(gr_ref[...], approximate=True)
    inv = lax.rsqrt(jnp.mean(rec * rec, axis=-1, keepdims=True) + NORM_EPS)
    o_ref[...] = (rec * inv * g_ref[...]).astype(o_ref.dtype)


def _scan_bwd_finish(a, u, hf, xg, norm_g, batch, seq):
    n, c = a.shape
    t = _tile(seq, 256)
    nt = seq // t
    rev = lambda b, j: (b * nt + (nt - 1 - j), 0)
    spec = pl.BlockSpec((t, c), rev)
    return pl.pallas_call(
        functools.partial(_scan_bwd_kernel, t=t),
        grid=(batch, nt),
        in_specs=[spec, spec, spec,
                  pl.BlockSpec((t, c), lambda b, j: (b * nt + (nt - 1 - j), 1)),
                  pl.BlockSpec((1, c), lambda b, j: (0, 0))],
        out_specs=spec,
        out_shape=jax.ShapeDtypeStruct((n, c), BF16),
        scratch_shapes=[pltpu.VMEM((1, c), F32), pltpu.VMEM((t, c), F32)],
        compiler_params=_params("arbitrary", "arbitrary"),
        name="scan_bwd",
    )(a, u, hf, xg, norm_g.reshape(1, c))


def _router_kernel(h_ref, g_ref, rw_ref, hn_ref, aff_ref, *, n_experts):
    x = h_ref[...]
    inv = lax.rsqrt(jnp.mean(x * x, axis=-1, keepdims=True) + NORM_EPS)
    hn = x * inv * g_ref[...]
    hn_ref[...] = hn
    logits = jnp.dot(hn.astype(BF16), rw_ref[...], preferred_element_type=F32)
    lane = lax.broadcasted_iota(jnp.int32, logits.shape, 1)
    logits = jnp.where(lane < n_experts, logits, -jnp.inf)
    e = jnp.exp(logits - jnp.max(logits, axis=-1, keepdims=True))
    aff_ref[...] = e / jnp.sum(e, axis=-1, keepdims=True)


def _router(h, g, rw_pad, n_experts):
    n, d = h.shape
    tm = _tile(n, 256)
    return pl.pallas_call(
        functools.partial(_router_kernel, n_experts=n_experts),
        grid=(n // tm,),
        in_specs=[pl.BlockSpec((tm, d), lambda i: (i, 0)),
                  pl.BlockSpec((1, d), lambda i: (0, 0)),
                  pl.BlockSpec((d, LANES), lambda i: (0, 0))],
        out_specs=[pl.BlockSpec((tm, d), lambda i: (i, 0)),
                   pl.BlockSpec((tm, LANES), lambda i: (i, 0))],
        out_shape=[jax.ShapeDtypeStruct((n, d), F32), jax.ShapeDtypeStruct((n, LANES), F32)],
        compiler_params=_params("arbitrary"),
        name="router",
    )(h, g.reshape(1, d), rw_pad)


def _route_kernel(aff_ref, dest_ref, *, cap, n_tok, chunk):
    n_chunks = n_tok // chunk

    def bits_of(c):
        return lax.bitcast_convert_type(aff_ref[pl.ds(pl.multiple_of(c * chunk, chunk), chunk), :],
                                        jnp.int32)

    def count(pred):
        def body(c, acc):
            return acc + jnp.sum(jnp.where(pred(bits_of(c)), 1.0, 0.0), axis=0, keepdims=True)
        return lax.fori_loop(0, n_chunks, body, jnp.zeros((1, LANES), F32))

    def search(i, thr):
        cand = thr | lax.shift_left(jnp.int32(1), 30 - i)
        return jnp.where(count(lambda v: v >= cand) >= cap, cand, thr)

    thr = lax.fori_loop(0, 31, search, jnp.zeros((1, LANES), jnp.int32))
    need = cap - count(lambda v: v > thr)

    ltri = jnp.where(lax.broadcasted_iota(jnp.int32, (chunk, chunk), 0)
                     > lax.broadcasted_iota(jnp.int32, (chunk, chunk), 1), 1.0, 0.0).astype(BF16)

    def place(c, carry):
        ties_before, taken_before = carry
        v = bits_of(c)
        tie = v == thr
        tie_f = jnp.where(tie, 1.0, 0.0)
        tie_rank = jnp.dot(ltri, tie_f.astype(BF16), preferred_element_type=F32) + ties_before
        take = jnp.logical_or(v > thr, jnp.logical_and(tie, tie_rank < need))
        take_f = jnp.where(take, 1.0, 0.0)
        pos = jnp.dot(ltri, take_f.astype(BF16), preferred_element_type=F32) + taken_before
        dest_ref[pl.ds(pl.multiple_of(c * chunk, chunk), chunk), :] = jnp.where(
            take, pos.astype(jnp.int32), cap)
        return (ties_before + jnp.sum(tie_f, axis=0, keepdims=True),
                taken_before + jnp.sum(take_f, axis=0, keepdims=True))

    zero = jnp.zeros((1, LANES), F32)
    lax.fori_loop(0, n_chunks, place, (zero, zero))


def _route(aff_pad, cap):
    n_tok = aff_pad.shape[0]
    chunk = _tile(n_tok, LANES)
    return pl.pallas_call(
        functools.partial(_route_kernel, cap=cap, n_tok=n_tok, chunk=chunk),
        out_shape=jax.ShapeDtypeStruct((n_tok, LANES), jnp.int32),
        compiler_params=pltpu.CompilerParams(vmem_limit_bytes=VMEM_LIMIT_BYTES),
        name="route",
    )(aff_pad)


def _compact_kernel(dest_ref, bits_ref, idx_ref, gate_ref, *, n_tok, cap):
    def clear(i, c):
        idx_ref[i] = 0
        gate_ref[i] = 0
        return c

    lax.fori_loop(cap, idx_ref.shape[0], clear, 0)

    def body(n, c):
        d = dest_ref[n]
        idx_ref[d] = n
        gate_ref[d] = bits_ref[n]
        return c

    lax.fori_loop(0, n_tok, body, 0, unroll=16)


def _compact(dest_flat, bits_flat, n_experts, n_tok, cap):
    width = cap + SMEM_TILE_1D
    in_spec = pl.BlockSpec((n_tok,), lambda e: (e,), memory_space=pltpu.SMEM)
    out_spec = pl.BlockSpec((width,), lambda e: (e,), memory_space=pltpu.SMEM)
    out = jax.ShapeDtypeStruct((n_experts * width,), jnp.int32)
    idx, gate = pl.pallas_call(
        functools.partial(_compact_kernel, n_tok=n_tok, cap=cap),
        grid=(n_experts,),
        in_specs=[in_spec, in_spec],
        out_specs=[out_spec, out_spec],
        out_shape=[out, out],
        compiler_params=_params("arbitrary"),
        name="compact",
    )(dest_flat, bits_flat)
    return idx.reshape(n_experts, width), gate.reshape(n_experts, width)


def _row_copy(src, dst, src_row, dst_row, sem):
    return pltpu.make_async_copy(src.at[pl.ds(src_row, 1), :], dst.at[pl.ds(dst_row, 1), :], sem)


def _gather_kernel(idx_ref, src_ref, o_ref, buf, sem, *, tg):
    base = pl.program_id(0) * tg

    def start(r, c):
        _row_copy(src_ref, buf, idx_ref[base + r], r, sem).start()
        return c

    def wait(r, c):
        _row_copy(src_ref, buf, 0, r, sem).wait()
        return c

    lax.fori_loop(0, tg, start, 0)
    lax.fori_loop(0, tg, wait, 0)
    o_ref[...] = buf[...].astype(o_ref.dtype)


def _gather_rows(src, idx, out_dtype):
    n_slots = idx.shape[0]
    d = src.shape[1]
    tg = _tile(n_slots, 256)
    grid_spec = pltpu.PrefetchScalarGridSpec(
        num_scalar_prefetch=1,
        grid=(n_slots // tg,),
        in_specs=[pl.BlockSpec(memory_space=pl.ANY)],
        out_specs=pl.BlockSpec((tg, d), lambda i, s: (i, 0)),
        scratch_shapes=[pltpu.VMEM((tg, d), src.dtype), pltpu.SemaphoreType.DMA],
    )
    return pl.pallas_call(
        functools.partial(_gather_kernel, tg=tg),
        grid_spec=grid_spec,
        out_shape=jax.ShapeDtypeStruct((n_slots, d), out_dtype),
        compiler_params=_params("arbitrary"),
        name="gather_rows",
    )(idx, src)


def _scatter_add_kernel(idx_ref, ye_ref, h_in_ref, h_ref, buf, sem, *, tg):
    del h_in_ref
    base = pl.program_id(0) * tg

    def start_in(r, c):
        _row_copy(h_ref, buf, idx_ref[base + r], r, sem).start()
        return c

    def wait_in(r, c):
        _row_copy(h_ref, buf, 0, r, sem).wait()
        return c

    def start_out(r, c):
        _row_copy(buf, h_ref, r, idx_ref[base + r], sem).start()
        return c

    def wait_out(r, c):
        _row_copy(buf, h_ref, r, 0, sem).wait()
        return c

    lax.fori_loop(0, tg, start_in, 0)
    lax.fori_loop(0, tg, wait_in, 0)
    buf[...] = buf[...] + ye_ref[...]
    lax.fori_loop(0, tg, start_out, 0)
    lax.fori_loop(0, tg, wait_out, 0)


def _scatter_add(h, ye, idx, cap):
    n_slots, d = ye.shape
    tg = _tile(cap, 256)
    grid_spec = pltpu.PrefetchScalarGridSpec(
        num_scalar_prefetch=1,
        grid=(n_slots // tg,),
        in_specs=[pl.BlockSpec((tg, d), lambda i, s: (i, 0)),
                  pl.BlockSpec(memory_space=pl.ANY)],
        out_specs=pl.BlockSpec(memory_space=pl.ANY),
        scratch_shapes=[pltpu.VMEM((tg, d), F32), pltpu.SemaphoreType.DMA],
    )
    return pl.pallas_call(
        functools.partial(_scatter_add_kernel, tg=tg),
        grid_spec=grid_spec,
        out_shape=jax.ShapeDtypeStruct(h.shape, h.dtype),
        input_output_aliases={2: 0},
        compiler_params=_params("arbitrary"),
        name="scatter_add",
    )(idx, ye, h)


def _ffn_up_kernel(x_ref, w1_ref, w3_ref, o_ref, w1_sc, w3_sc):
    @pl.when(pl.program_id(2) == 0)
    def _():
        w1_sc[...] = w1_ref[...].astype(BF16)
        w3_sc[...] = w3_ref[...].astype(BF16)

    x = x_ref[...]
    a = jnp.dot(x, w1_sc[...], preferred_element_type=F32)
    b = jnp.dot(x, w3_sc[...], preferred_element_type=F32)
    o_ref[...] = (jax.nn.silu(a) * b).astype(o_ref.dtype)


def _ffn_up(xe, w1, w3, cap, tm=1024, tn=256):
    n_experts, d, f = w1.shape
    tm, tn = _tile(cap, tm), _tile(f, tn)
    nm = cap // tm
    wspec = pl.BlockSpec((None, d, tn), lambda e, j, i: (e, 0, j))
    return pl.pallas_call(
        _ffn_up_kernel,
        grid=(n_experts, f // tn, nm),
        in_specs=[pl.BlockSpec((tm, d), lambda e, j, i: (e * nm + i, 0)), wspec, wspec],
        out_specs=pl.BlockSpec((tm, tn), lambda e, j, i: (e * nm + i, j)),
        out_shape=jax.ShapeDtypeStruct((n_experts * cap, f), BF16),
        scratch_shapes=[pltpu.VMEM((d, tn), BF16), pltpu.VMEM((d, tn), BF16)],
        compiler_params=_params("arbitrary", "arbitrary", "arbitrary"),
        name="ffn_up",
    )(xe, w1, w3)


def _ffn_down_kernel(h_ref, w2_ref, gate_ref, o_ref, w2_sc):
    @pl.when(pl.program_id(2) == 0)
    def _():
        w2_sc[...] = w2_ref[...].astype(BF16)

    o_ref[...] = jnp.dot(h_ref[...], w2_sc[...], preferred_element_type=F32) * gate_ref[...]


def _ffn_down(hm, w2, gates, cap, tm=1024, tn=256):
    n_experts, f, d = w2.shape
    tm, tn = _tile(cap, tm), _tile(d, tn)
    nm = cap // tm
    return pl.pallas_call(
        _ffn_down_kernel,
        grid=(n_experts, d // tn, nm),
        in_specs=[pl.BlockSpec((tm, f), lambda e, j, i: (e * nm + i, 0)),
                  pl.BlockSpec((None, f, tn), lambda e, j, i: (e, 0, j)),
                  pl.BlockSpec((tm, 1), lambda e, j, i: (e * nm + i, 0))],
        out_specs=pl.BlockSpec((tm, tn), lambda e, j, i: (e * nm + i, j)),
        out_shape=jax.ShapeDtypeStruct((n_experts * cap, d), F32),
        scratch_shapes=[pltpu.VMEM((f, tn), BF16)],
        compiler_params=_params("arbitrary", "arbitrary", "arbitrary"),
        name="ffn_down",
    )(hm, w2, gates)


def _prepare_weights(rg_wa, rg_ba, rg_wx, rg_bx, router_w):
    n_blocks, blk = rg_wa.shape[1], rg_wa.shape[2]
    wg = jnp.concatenate([rg_wa[0], rg_wa[1], rg_wx[0], rg_wx[1]], axis=-1).astype(BF16)
    bg = jnp.concatenate([rg_ba[0].reshape(n_blocks, blk), rg_ba[1].reshape(n_blocks, blk),
                          rg_bx[0].reshape(n_blocks, blk), rg_bx[1].reshape(n_blocks, blk)], axis=-1)
    n_experts = router_w.shape[1]
    rw_pad = jnp.pad(router_w, ((0, 0), (0, LANES - n_experts))).astype(BF16)
    return dict(wg=wg, bg=bg, rw_pad=rw_pad)


def _layer(x, layer, pw, norm_mix_g, w_in, lambda_q1, lambda_k1, lambda_q2, lambda_k2,
           attn_subln_g, conv_w, conv_b, rg_lambda, rec_norm_g, w_out, norm_ffn_g,
           exp_w1, exp_w3, exp_w2, n_experts, attn_width):
    batch, seq, d = x.shape
    n = batch * seq
    n_heads = attn_width // V_HEAD_DIM
    rec_width = conv_w.shape[-1]
    x2 = x.reshape(n, d)

    xn = _rmsnorm(x2, norm_mix_g, BF16)
    q_scale = QK_HALF_DIM ** -0.5 * LOG2E
    qkv_scale = jnp.concatenate([jnp.full((1, attn_width), q_scale, F32),
                                 jnp.ones((1, 2 * attn_width), F32)], axis=1)
    qkv = _matmul(xn, w_in, 0, 3 * attn_width, qkv_scale, BF16)
    xg = _matmul(xn, w_in, 3 * attn_width, 2 * rec_width, jnp.ones((1, 2 * rec_width), F32), F32)
    lam_init = 0.8 - 0.6 * math.exp(-0.3 * layer)
    slopes = 2.0 ** (-8.0 * jnp.arange(1, n_heads + 1, dtype=F32) / n_heads)
    attn = _diff_attention(qkv, batch, seq, n_heads, slopes, lambda_q1, lambda_k1, lambda_q2,
                           lambda_k2, attn_subln_g, lam_init)
    a_f, u_f, a_b, u_b = _rg_gates(xg, batch, seq, conv_w, conv_b, pw["wg"], pw["bg"], rg_lambda)
    h_f = _scan_fwd(a_f, u_f, batch, seq)
    rec = _scan_bwd_finish(a_b, u_b, h_f, xg, rec_norm_g, batch, seq)
    h = _outproj(attn, rec, w_out, x2)

    cap = (EC_CAPACITY_FACTOR * n) // n_experts
    hn, aff = _router(h, norm_ffn_g, pw["rw_pad"], n_experts)
    dest = _route(aff, cap)
    bits_t = lax.bitcast_convert_type(aff[:, :n_experts].T, jnp.int32)
    idx, gate_bits = _compact(dest[:, :n_experts].T.reshape(-1), bits_t.reshape(-1),
                              n_experts, n, cap)
    idx = idx[:, :cap].reshape(-1)
    gates = lax.bitcast_convert_type(gate_bits[:, :cap], F32).reshape(-1, 1)
    xe = _gather_rows(hn, idx, BF16)
    hm = _ffn_up(xe, exp_w1, exp_w3, cap)
    ye = _ffn_down(hm, exp_w2, gates, cap)
    return _scatter_add(h, ye, idx, cap)


def _trunk(x, pw, norm_mix_g, w_in, lambda_q1, lambda_k1, lambda_q2, lambda_k2, attn_subln_g,
           conv_w, conv_b, rg_lambda, rec_norm_g, w_out, norm_ffn_g, exp_w1, exp_w3, exp_w2,
           final_norm_g, n_experts, attn_width):
    batch, seq, d = x.shape
    h = x
    for l in range(norm_mix_g.shape[0]):
        h = _layer(h.reshape(batch, seq, d), l, pw[l], norm_mix_g[l], w_in[l], lambda_q1[l],
                   lambda_k1[l], lambda_q2[l], lambda_k2[l], attn_subln_g[l], conv_w[l], conv_b[l],
                   rg_lambda[l], rec_norm_g[l], w_out[l], norm_ffn_g[l], exp_w1[l], exp_w3[l],
                   exp_w2[l], n_experts, attn_width)
    return _rmsnorm(h.reshape(batch * seq, d), final_norm_g, F32).reshape(batch, seq, d)


def kernel(x_prompt, x_sample, norm_mix_g, w_in, lambda_q1, lambda_k1, lambda_q2, lambda_k2, attn_subln_g, conv_w, conv_b, rg_wa, rg_ba, rg_wx, rg_bx, rg_lambda, rec_norm_g, w_out, norm_ffn_g, router_w, exp_w1, exp_w3, exp_w2, final_norm_g):
    depth = w_in.shape[0]
    rec_width = conv_w.shape[-1]
    attn_width = (w_in.shape[-1] - 2 * rec_width) // 3
    n_experts = router_w.shape[-1]
    pw = [_prepare_weights(rg_wa[l], rg_ba[l], rg_wx[l], rg_bx[l], router_w[l]) for l in range(depth)]
    run = lambda x: _trunk(x, pw, norm_mix_g, w_in, lambda_q1, lambda_k1, lambda_q2, lambda_k2,
                           attn_subln_g, conv_w, conv_b, rg_lambda, rec_norm_g, w_out, norm_ffn_g,
                           exp_w1, exp_w3, exp_w2, final_norm_g, n_experts, attn_width)
    return (run(x_prompt), run(x_sample))
```

```python
import functools
import math

import jax
import jax.numpy as jnp
from jax import lax
from jax.experimental import pallas as pl
from jax.experimental.pallas import tpu as pltpu

V_HEAD_DIM = 128
QK_HALF_DIM = V_HEAD_DIM // 2
CONV_WIDTH = 4
RG_C = 8.0
EC_CAPACITY_FACTOR = 2
NORM_EPS = 1e-6
LANES = 128
SUBLANES = 8
SMEM_TILE_1D = 1024
VMEM_LIMIT_BYTES = 52 * 1024 * 1024

GATHER_ROWS = 512
SCATTER_ROWS = 256
ATTN_TQ = 256
ATTN_TK = 1024
LOG2E = math.log2(math.e)
SKIP_LOG2 = 50.0
FAST_LOG2 = 110.0
NORM_SLACK = 1.01

F32 = jnp.float32
BF16 = jnp.bfloat16


def _params(*sem):
    return pltpu.CompilerParams(dimension_semantics=sem, vmem_limit_bytes=VMEM_LIMIT_BYTES)


def _tile(n, want):
    t = min(n, want)
    while n % t:
        t //= 2
    return t


def _rms_kernel(x_ref, g_ref, o_ref):
    x = x_ref[...]
    inv = lax.rsqrt(jnp.mean(x * x, axis=-1, keepdims=True) + NORM_EPS)
    o_ref[...] = (x * inv * g_ref[...]).astype(o_ref.dtype)


def _rmsnorm(x, g, out_dtype):
    n, d = x.shape
    tm = _tile(n, 256)
    return pl.pallas_call(
        _rms_kernel,
        grid=(n // tm,),
        in_specs=[pl.BlockSpec((tm, d), lambda i: (i, 0)),
                  pl.BlockSpec((1, d), lambda i: (0, 0))],
        out_specs=pl.BlockSpec((tm, d), lambda i: (i, 0)),
        out_shape=jax.ShapeDtypeStruct((n, d), out_dtype),
        compiler_params=_params("arbitrary"),
        name="rmsnorm",
    )(x, g.reshape(1, d))


def _mm_kernel(x_ref, w_ref, s_ref, o_ref, wb_sc):
    @pl.when(pl.program_id(1) == 0)
    def _():
        wb_sc[...] = w_ref[...].astype(BF16)

    acc = jnp.dot(x_ref[...], wb_sc[...], preferred_element_type=F32)
    o_ref[...] = (acc * s_ref[...]).astype(o_ref.dtype)


def _matmul(x, w, col0, n, col_scale, out_dtype, tm=1024, tn=512):
    m, k = x.shape
    tm, tn = _tile(m, tm), _tile(math.gcd(n, col0), tn)
    jb = col0 // tn
    return pl.pallas_call(
        _mm_kernel,
        grid=(n // tn, m // tm),
        in_specs=[pl.BlockSpec((tm, k), lambda j, i: (i, 0)),
                  pl.BlockSpec((k, tn), lambda j, i: (0, jb + j)),
                  pl.BlockSpec((1, tn), lambda j, i: (0, j))],
        out_specs=pl.BlockSpec((tm, tn), lambda j, i: (i, j)),
        out_shape=jax.ShapeDtypeStruct((m, n), out_dtype),
        scratch_shapes=[pltpu.VMEM((k, tn), BF16)],
        compiler_params=_params("arbitrary", "arbitrary"),
        name="in_proj",
    )(x, w, col_scale)


def _outproj_kernel(a_ref, r_ref, wt_ref, wb_ref, x_ref, o_ref, wt_sc, wb_sc):
    @pl.when(pl.program_id(1) == 0)
    def _():
        wt_sc[...] = wt_ref[...].astype(BF16)
        wb_sc[...] = wb_ref[...].astype(BF16)

    acc = jnp.dot(a_ref[...], wt_sc[...], preferred_element_type=F32)
    acc = acc + jnp.dot(r_ref[...], wb_sc[...], preferred_element_type=F32)
    o_ref[...] = x_ref[...] + acc


def _outproj(attn, rec, w, x, tm=1024, tn=512):
    m, ka = attn.shape
    kr = rec.shape[1]
    assert ka == kr
    n = w.shape[1]
    tm, tn = _tile(m, tm), _tile(n, tn)
    return pl.pallas_call(
        _outproj_kernel,
        grid=(n // tn, m // tm),
        in_specs=[pl.BlockSpec((tm, ka), lambda j, i: (i, 0)),
                  pl.BlockSpec((tm, kr), lambda j, i: (i, 0)),
                  pl.BlockSpec((ka, tn), lambda j, i: (0, j)),
                  pl.BlockSpec((kr, tn), lambda j, i: (1, j)),
                  pl.BlockSpec((tm, tn), lambda j, i: (i, j))],
        out_specs=pl.BlockSpec((tm, tn), lambda j, i: (i, j)),
        out_shape=jax.ShapeDtypeStruct((m, n), F32),
        scratch_shapes=[pltpu.VMEM((ka, tn), BF16), pltpu.VMEM((kr, tn), BF16)],
        compiler_params=_params("arbitrary", "arbitrary"),
        name="out_proj",
    )(attn, rec, w, w, x)


def _qk_bound_kernel(q_ref, k_ref, g_ref, o_ref):
    @pl.when(pl.program_id(1) == 0)
    def _():
        o_ref[...] = jnp.zeros(o_ref.shape, F32)

    def group_max(x_ref):
        x = x_ref[...].astype(F32)
        sq = x * x
        hi = sq.astype(BF16)
        lo = (sq - hi.astype(F32)).astype(BF16)
        s = (jnp.dot(hi, g_ref[...], preferred_element_type=F32)
             + jnp.dot(lo, g_ref[...], preferred_element_type=F32))
        return jnp.max(s, axis=0, keepdims=True)

    o_ref[0:1, :] = jnp.maximum(o_ref[0:1, :], group_max(q_ref))
    o_ref[1:2, :] = jnp.maximum(o_ref[1:2, :], group_max(k_ref))


def _qk_bounds(qkv, batch, seq, attn_width):
    n_groups = attn_width // QK_HALF_DIM
    assert n_groups <= LANES
    tm = _tile(seq, 512)
    nt = seq // tm
    g = (jnp.arange(attn_width)[:, None] // QK_HALF_DIM == jnp.arange(LANES)[None, :]).astype(BF16)
    return pl.pallas_call(
        _qk_bound_kernel,
        grid=(batch, nt),
        in_specs=[pl.BlockSpec((tm, attn_width), lambda b, j: (b * nt + j, 0)),
                  pl.BlockSpec((tm, attn_width), lambda b, j: (b * nt + j, 1)),
                  pl.BlockSpec((attn_width, LANES), lambda b, j: (0, 0))],
        out_specs=pl.BlockSpec((None, 2, LANES), lambda b, j: (b, 0, 0)),
        out_shape=jax.ShapeDtypeStruct((batch, 2, LANES), F32),
        compiler_params=_params("arbitrary", "arbitrary"),
        name="qk_bounds",
    )(qkv, qkv, g)


def _attn_kernel(slopes_ref, band_ref, fast_ref, kmax_ref, q_ref, k_ref, v_ref, lq1_ref, lk1_ref,
                 lq2_ref, lk2_ref, g_ref, o_ref, m_sc, l_sc, acc_sc, lp_sc,
                 *, tq, tk, seq, n_heads, lam_init):
    b = pl.program_id(0)
    h = pl.program_id(1)
    bh = b * n_heads + h
    slope = slopes_ref[h]
    band = band_ref[bh]
    nk = seq // tk
    nch = tk // LANES
    q0 = pl.program_id(2) * tq
    kd = lax.div(q0, tk)
    lo = jnp.maximum(kd - band, 0)
    hi = jnp.minimum(kd + 1 + band, nk)

    q = q_ref[...]
    lane = lax.broadcasted_iota(jnp.int32, (tq, V_HEAD_DIM), 1)
    zero = jnp.zeros_like(q)
    qm = jnp.concatenate([jnp.where(lane < QK_HALF_DIM, q, zero),
                          jnp.where(lane >= QK_HALF_DIM, q, zero)], axis=0)
    d0 = (lax.broadcasted_iota(jnp.int32, (tq, tk), 0)
          - lax.broadcasted_iota(jnp.int32, (tq, tk), 1))

    def scores(k0):
        return lax.dot_general(qm, k_ref[pl.ds(k0, tk), :], (((1,), (1,)), ((), ())),
                               preferred_element_type=F32)

    def abs_bias(k0):
        bias = slope * jnp.abs(d0 + (q0 - k0)).astype(F32)
        return jnp.concatenate([bias, bias], axis=0)

    def finalize(l):
        lam = (jnp.exp(jnp.sum(lq1_ref[...] * lk1_ref[...], axis=1, keepdims=True))
               - jnp.exp(jnp.sum(lq2_ref[...] * lk2_ref[...], axis=1, keepdims=True))
               + lam_init)
        o = acc_sc[...] / l
        o = o[:tq] - lam * o[tq:]
        inv = lax.rsqrt(jnp.mean(o * o, axis=-1, keepdims=True) + NORM_EPS)
        o_ref[...] = ((o * inv * g_ref[...]) * (1.0 - lam_init)).astype(o_ref.dtype)

    @pl.when(fast_ref[bh] != 0)
    def _():
        qf = qm.astype(F32)
        nq = jnp.sqrt(jnp.sum(qf * qf, axis=1, keepdims=True))
        row = lax.broadcasted_iota(jnp.int32, (2 * tq, 1), 0)
        km = jnp.where(row < tq, kmax_ref[2 * bh], kmax_ref[2 * bh + 1])
        m_rep = jnp.broadcast_to(nq * km, (2 * tq, LANES))
        il = lax.broadcasted_iota(jnp.int32, (2 * tq, LANES), 0)
        il = slope * jnp.where(il >= tq, il - tq, il).astype(F32)
        jl = lax.broadcasted_iota(jnp.int32, (1, tk), 1).astype(F32)
        col_l = slope * (jl - (tk - 1))
        col_r = -slope * jl
        acc_sc[...] = jnp.zeros(acc_sc.shape, F32)
        lp_sc[...] = jnp.zeros(lp_sc.shape, F32)

        def tiled(x):
            return jnp.concatenate([x] * nch, axis=1)

        def accumulate(e, k0):
            p = jnp.exp2(e)
            lp = lp_sc[...]
            for c in range(nch):
                lp = lp + p[:, c * LANES:(c + 1) * LANES]
            lp_sc[...] = lp
            acc_sc[...] += jnp.dot(p.astype(BF16), v_ref[pl.ds(k0, tk), :],
                                   preferred_element_type=F32)

        k0d = pl.multiple_of(kd * tk, tk)
        accumulate((scores(k0d) - abs_bias(k0d)) - tiled(m_rep), k0d)

        def left(kj, c):
            k0 = pl.multiple_of(kj * tk, tk)
            mrow = (m_rep + il) - slope * (k0 + (tk - 1) - q0).astype(F32)
            accumulate((scores(k0) + col_l) - tiled(mrow), k0)
            return c

        def right(kj, c):
            k0 = pl.multiple_of(kj * tk, tk)
            mrow = (m_rep - il) - slope * (q0 - k0).astype(F32)
            accumulate((scores(k0) + col_r) - tiled(mrow), k0)
            return c

        lax.fori_loop(lo, kd, left, 0)
        lax.fori_loop(kd + 1, hi, right, 0)
        finalize(jnp.sum(lp_sc[...], axis=1, keepdims=True))

    @pl.when(fast_ref[bh] == 0)
    def _():
        m_sc[...] = jnp.full(m_sc.shape, -jnp.inf, F32)
        l_sc[...] = jnp.zeros(l_sc.shape, F32)
        acc_sc[...] = jnp.zeros(acc_sc.shape, F32)

        def body(kj, c):
            k0 = pl.multiple_of(kj * tk, tk)
            s = scores(k0) - abs_bias(k0)
            m_prev = m_sc[...]
            m_new = jnp.maximum(m_prev, jnp.max(s, axis=1, keepdims=True))
            alpha = jnp.exp2(m_prev - m_new)
            p = jnp.exp2(s - m_new)
            l_sc[...] = alpha * l_sc[...] + jnp.sum(p, axis=1, keepdims=True)
            acc_sc[...] = alpha * acc_sc[...] + jnp.dot(p.astype(BF16), v_ref[pl.ds(k0, tk), :],
                                                         preferred_element_type=F32)
            m_sc[...] = m_new
            return c

        lax.fori_loop(lo, hi, body, 0)
        finalize(l_sc[...])


def _diff_attention(qkv, batch, seq, n_heads, slopes, lq1, lk1, lq2, lk2, subln_g, lam_init):
    n = batch * seq
    attn_width = n_heads * V_HEAD_DIM
    tq = _tile(seq, ATTN_TQ)
    tk = _tile(seq, ATTN_TK)
    assert tk % tq == 0 and tk % LANES == 0
    nq = seq // tq

    sq = _qk_bounds(qkv, batch, seq, attn_width)
    norms = jnp.sqrt(sq[:, :, :2 * n_heads]).reshape(batch, 2, n_heads, 2) * NORM_SLACK
    qmax, kmax = norms[:, 0], norms[:, 1]
    bmax = jnp.max(qmax * kmax, axis=-1)
    slopes2 = slopes * LOG2E
    band = jnp.floor((2.0 * bmax + SKIP_LOG2) / (slopes2[None, :] * tk)) + 1.0
    band = jnp.minimum(band, seq // tk).astype(jnp.int32).reshape(-1)
    fast = (2.0 * bmax <= FAST_LOG2).astype(jnp.int32).reshape(-1)

    kernel = functools.partial(_attn_kernel, tq=tq, tk=tk, seq=seq, n_heads=n_heads,
                               lam_init=lam_init)
    vec = lambda v: v.reshape(1, -1).astype(F32)
    small = lambda w: pl.BlockSpec((1, w), lambda b, h, i, *_: (0, 0))
    grid_spec = pltpu.PrefetchScalarGridSpec(
        num_scalar_prefetch=4,
        grid=(batch, n_heads, nq),
        in_specs=[pl.BlockSpec((tq, V_HEAD_DIM), lambda b, h, i, *_: (b * nq + i, h)),
                  pl.BlockSpec((seq, V_HEAD_DIM), lambda b, h, i, *_: (b, n_heads + h)),
                  pl.BlockSpec((seq, V_HEAD_DIM), lambda b, h, i, *_: (b, 2 * n_heads + h)),
                  small(QK_HALF_DIM), small(QK_HALF_DIM), small(QK_HALF_DIM), small(QK_HALF_DIM),
                  small(V_HEAD_DIM)],
        out_specs=pl.BlockSpec((tq, V_HEAD_DIM), lambda b, h, i, *_: (b * nq + i, h)),
        scratch_shapes=[pltpu.VMEM((2 * tq, 1), F32), pltpu.VMEM((2 * tq, 1), F32),
                        pltpu.VMEM((2 * tq, V_HEAD_DIM), F32), pltpu.VMEM((2 * tq, LANES), F32)],
    )
    return pl.pallas_call(
        kernel,
        grid_spec=grid_spec,
        out_shape=jax.ShapeDtypeStruct((n, attn_width), BF16),
        compiler_params=_params("arbitrary", "arbitrary", "arbitrary"),
        name="diff_attn",
    )(slopes2, band, fast, kmax.reshape(-1), qkv, qkv, qkv,
      vec(lq1), vec(lk1), vec(lq2), vec(lk2), vec(subln_g))


def _gates_kernel(cur_ref, prev_ref, next_ref, cw_ref, cb_ref, wg_ref, bg_ref, lam_ref,
                  af_ref, uf_ref, ab_ref, ub_ref, ext_sc, *, t, n_blocks, blk):
    j = pl.program_id(1)
    nt = pl.num_programs(1)
    halo = SUBLANES
    ext_sc[0:halo, :] = jnp.where(j > 0, prev_ref[...], 0.0)
    ext_sc[halo:halo + t, :] = cur_ref[...]
    ext_sc[halo + t:2 * halo + t, :] = jnp.where(j < nt - 1, next_ref[...], 0.0)
    left = CONV_WIDTH // 2
    y = cb_ref[...]
    for c in range(CONV_WIDTH):
        y = y + ext_sc[halo - left + c:halo - left + c + t, :] * cw_ref[c:c + 1, :]
    nl = -lam_ref[...]
    sp = jnp.maximum(nl, 0.0) + jnp.log1p(jnp.exp(-jnp.abs(nl)))
    outs = ((af_ref, uf_ref), (ab_ref, ub_ref))
    for g in range(n_blocks):
        cs = slice(g * blk, (g + 1) * blk)
        xc = y[:, cs]
        pre = jnp.dot(xc.astype(BF16), wg_ref[g], preferred_element_type=F32) + bg_ref[g:g + 1, :]
        for d in range(2):
            r = jax.nn.sigmoid(pre[:, d * blk:(d + 1) * blk])
            i = jax.nn.sigmoid(pre[:, (2 + d) * blk:(3 + d) * blk])
            log_a = (-RG_C * r) * sp[d:d + 1, cs]
            a = jnp.exp(log_a)
            u = jnp.sqrt(1.0 - jnp.exp(2.0 * log_a)) * (i * xc)
            outs[d][0][:, cs] = a
            outs[d][1][:, cs] = u


def _rg_gates(xg, batch, seq, conv_w, conv_b, wg, bg, lam):
    n = batch * seq
    c = xg.shape[1] // 2
    n_blocks, blk = wg.shape[0], wg.shape[1]
    t = _tile(seq, 256)
    nt = seq // t
    hb = t // SUBLANES
    last = n // SUBLANES - 1
    kernel = functools.partial(_gates_kernel, t=t, n_blocks=n_blocks, blk=blk)
    full = lambda shape: pl.BlockSpec(shape, lambda b, j: (0,) * len(shape))
    out_spec = pl.BlockSpec((t, c), lambda b, j: (b * nt + j, 0))
    return pl.pallas_call(
        kernel,
        grid=(batch, nt),
        in_specs=[pl.BlockSpec((t, c), lambda b, j: (b * nt + j, 0)),
                  pl.BlockSpec((SUBLANES, c), lambda b, j: (jnp.maximum((b * nt + j) * hb - 1, 0), 0)),
                  pl.BlockSpec((SUBLANES, c), lambda b, j: (jnp.minimum((b * nt + j + 1) * hb, last), 0)),
                  full((CONV_WIDTH, c)), full((1, c)),
                  full(wg.shape), full(bg.shape), full((2, c))],
        out_specs=[out_spec] * 4,
        out_shape=[jax.ShapeDtypeStruct((n, c), F32)] * 4,
        scratch_shapes=[pltpu.VMEM((t + 2 * SUBLANES, c), F32)],
        compiler_params=_params("arbitrary", "arbitrary"),
        name="rg_gates",
    )(xg, xg, xg, conv_w, conv_b.reshape(1, c), wg, bg, lam)


def _scan_fwd_kernel(a_ref, u_ref, h_ref, carry_sc, *, t):
    @pl.when(pl.program_id(1) == 0)
    def _():
        carry_sc[...] = jnp.zeros(carry_sc.shape, F32)

    def body(i, h):
        h = a_ref[pl.ds(i, 1), :] * h + u_ref[pl.ds(i, 1), :]
        h_ref[pl.ds(i, 1), :] = h
        return h

    carry_sc[...] = lax.fori_loop(0, t, body, carry_sc[...], unroll=8)


def _scan_fwd(a, u, batch, seq):
    n, c = a.shape
    t = _tile(seq, 256)
    nt = seq // t
    spec = pl.BlockSpec((t, c), lambda b, j: (b * nt + j, 0))
    return pl.pallas_call(
        functools.partial(_scan_fwd_kernel, t=t),
        grid=(batch, nt),
        in_specs=[spec, spec],
        out_specs=spec,
        out_shape=jax.ShapeDtypeStruct((n, c), F32),
        scratch_shapes=[pltpu.VMEM((1, c), F32)],
        compiler_params=_params("arbitrary", "arbitrary"),
        name="scan_fwd",
    )(a, u)


def _scan_bwd_kernel(a_ref, u_ref, hf_ref, gr_ref, g_ref, o_ref, carry_sc, hb_sc, *, t):
    @pl.when(pl.program_id(1) == 0)
    def _():
        carry_sc[...] = jnp.zeros(carry_sc.shape, F32)

    def body(i, h):
        r = t - 1 - i
        h = a_ref[pl.ds(r, 1), :] * h + u_ref[pl.ds(r, 1), :]
        hb_sc[pl.ds(r, 1), :] = h
        return h

    carry_sc[...] = lax.fori_loop(0, t, body, carry_sc[...], unroll=8)
    rec = (hf_ref[...] + hb_sc[...]) * jax.nn.gelu(gr_ref[...], approximate=True)
    inv = lax.rsqrt(jnp.mean(rec * rec, axis=-1, keepdims=True) + NORM_EPS)
    o_ref[...] = (rec * inv * g_ref[...]).astype(o_ref.dtype)


def _scan_bwd_finish(a, u, hf, xg, norm_g, batch, seq):
    n, c = a.shape
    t = _tile(seq, 256)
    nt = seq // t
    rev = lambda b, j: (b * nt + (nt - 1 - j), 0)
    spec = pl.BlockSpec((t, c), rev)
    return pl.pallas_call(
        functools.partial(_scan_bwd_kernel, t=t),
        grid=(batch, nt),
        in_specs=[spec, spec, spec,
                  pl.BlockSpec((t, c), lambda b, j: (b * nt + (nt - 1 - j), 1)),
                  pl.BlockSpec((1, c), lambda b, j: (0, 0))],
        out_specs=spec,
        out_shape=jax.ShapeDtypeStruct((n, c), BF16),
        scratch_shapes=[pltpu.VMEM((1, c), F32), pltpu.VMEM((t, c), F32)],
        compiler_params=_params("arbitrary", "arbitrary"),
        name="scan_bwd",
    )(a, u, hf, xg, norm_g.reshape(1, c))


def _router_kernel(h_ref, g_ref, rw_ref, hn_ref, aff_ref, *, n_experts):
    x = h_ref[...]
    inv = lax.rsqrt(jnp.mean(x * x, axis=-1, keepdims=True) + NORM_EPS)
    hn = x * inv * g_ref[...]
    hn_ref[...] = hn
    logits = jnp.dot(hn.astype(BF16), rw_ref[...], preferred_element_type=F32)
    lane = lax.broadcasted_iota(jnp.int32, logits.shape, 1)
    logits = jnp.where(lane < n_experts, logits, -jnp.inf)
    e = jnp.exp(logits - jnp.max(logits, axis=-1, keepdims=True))
    aff_ref[...] = e / jnp.sum(e, axis=-1, keepdims=True)


def _router(h, g, rw_pad, n_experts):
    n, d = h.shape
    tm = _tile(n, 256)
    return pl.pallas_call(
        functools.partial(_router_kernel, n_experts=n_experts),
        grid=(n // tm,),
        in_specs=[pl.BlockSpec((tm, d), lambda i: (i, 0)),
                  pl.BlockSpec((1, d), lambda i: (0, 0)),
                  pl.BlockSpec((d, LANES), lambda i: (0, 0))],
        out_specs=[pl.BlockSpec((tm, d), lambda i: (i, 0)),
                   pl.BlockSpec((tm, LANES), lambda i: (i, 0))],
        out_shape=[jax.ShapeDtypeStruct((n, d), F32), jax.ShapeDtypeStruct((n, LANES), F32)],
        compiler_params=_params("arbitrary"),
        name="router",
    )(h, g.reshape(1, d), rw_pad)


def _route_kernel(aff_ref, dest_ref, *, cap, n_tok, chunk):
    n_chunks = n_tok // chunk

    def bits_of(c):
        return lax.bitcast_convert_type(aff_ref[pl.ds(pl.multiple_of(c * chunk, chunk), chunk), :],
                                        jnp.int32)

    def count(pred):
        def body(c, acc):
            return acc + jnp.sum(jnp.where(pred(bits_of(c)), 1.0, 0.0), axis=0, keepdims=True)
        return lax.fori_loop(0, n_chunks, body, jnp.zeros((1, LANES), F32))

    def search(i, thr):
        cand = thr | lax.shift_left(jnp.int32(1), 30 - i)
        return jnp.where(count(lambda v: v >= cand) >= cap, cand, thr)

    thr = lax.fori_loop(0, 31, search, jnp.zeros((1, LANES), jnp.int32))
    need = cap - count(lambda v: v > thr)

    ltri = jnp.where(lax.broadcasted_iota(jnp.int32, (chunk, chunk), 0)
                     > lax.broadcasted_iota(jnp.int32, (chunk, chunk), 1), 1.0, 0.0).astype(BF16)

    def place(c, carry):
        ties_before, taken_before = carry
        v = bits_of(c)
        tie = v == thr
        tie_f = jnp.where(tie, 1.0, 0.0)
        tie_rank = jnp.dot(ltri, tie_f.astype(BF16), preferred_element_type=F32) + ties_before
        take = jnp.logical_or(v > thr, jnp.logical_and(tie, tie_rank < need))
        take_f = jnp.where(take, 1.0, 0.0)
        pos = jnp.dot(ltri, take_f.astype(BF16), preferred_element_type=F32) + taken_before
        dest_ref[pl.ds(pl.multiple_of(c * chunk, chunk), chunk), :] = jnp.where(
            take, pos.astype(jnp.int32), cap)
        return (ties_before + jnp.sum(tie_f, axis=0, keepdims=True),
                taken_before + jnp.sum(take_f, axis=0, keepdims=True))

    zero = jnp.zeros((1, LANES), F32)
    lax.fori_loop(0, n_chunks, place, (zero, zero))


def _route(aff_pad, cap):
    n_tok = aff_pad.shape[0]
    chunk = _tile(n_tok, LANES)
    return pl.pallas_call(
        functools.partial(_route_kernel, cap=cap, n_tok=n_tok, chunk=chunk),
        out_shape=jax.ShapeDtypeStruct((n_tok, LANES), jnp.int32),
        compiler_params=pltpu.CompilerParams(vmem_limit_bytes=VMEM_LIMIT_BYTES),
        name="route",
    )(aff_pad)


def _compact_kernel(dest_ref, bits_ref, idx_ref, gate_ref, *, n_tok, cap):
    def clear(i, c):
        idx_ref[i] = 0
        gate_ref[i] = 0
        return c

    lax.fori_loop(cap, idx_ref.shape[0], clear, 0)

    def body(n, c):
        d = dest_ref[n]
        idx_ref[d] = n
        gate_ref[d] = bits_ref[n]
        return c

    lax.fori_loop(0, n_tok, body, 0, unroll=16)


def _compact(dest_flat, bits_flat, n_experts, n_tok, cap):
    width = cap + SMEM_TILE_1D
    in_spec = pl.BlockSpec((n_tok,), lambda e: (e,), memory_space=pltpu.SMEM)
    out_spec = pl.BlockSpec((width,), lambda e: (e,), memory_space=pltpu.SMEM)
    out = jax.ShapeDtypeStruct((n_experts * width,), jnp.int32)
    idx, gate = pl.pallas_call(
        functools.partial(_compact_kernel, n_tok=n_tok, cap=cap),
        grid=(n_experts,),
        in_specs=[in_spec, in_spec],
        out_specs=[out_spec, out_spec],
        out_shape=[out, out],
        compiler_params=_params("arbitrary"),
        name="compact",
    )(dest_flat, bits_flat)
    return idx.reshape(n_experts, width), gate.reshape(n_experts, width)


def _row_copy(src, dst, src_row, dst_row, sem):
    return pltpu.make_async_copy(src.at[pl.ds(src_row, 1), :], dst.at[pl.ds(dst_row, 1), :], sem)


def _tile_wait(src, dst, sem, rows):
    pltpu.make_async_copy(src.at[pl.ds(0, rows), :], dst.at[pl.ds(0, rows), :], sem).wait()


def _gather_kernel(idx_ref, src_ref, o_ref, buf, sem, *, tg):
    i = pl.program_id(0)
    n = pl.num_programs(0)

    def issue(tile, slot):
        def start(r, c):
            _row_copy(src_ref, buf.at[slot], idx_ref[tile * tg + r], r, sem.at[slot]).start()
            return c
        lax.fori_loop(0, tg, start, 0, unroll=8)

    @pl.when(i == 0)
    def _():
        issue(0, 0)

    @pl.when(i + 1 < n)
    def _():
        issue(i + 1, (i + 1) % 2)

    slot = i % 2
    _tile_wait(src_ref, buf.at[slot], sem.at[slot], tg)
    o_ref[...] = buf[slot].astype(o_ref.dtype)


def _gather_rows(src, idx, out_dtype):
    n_slots = idx.shape[0]
    d = src.shape[1]
    tg = _tile(n_slots, GATHER_ROWS)
    grid_spec = pltpu.PrefetchScalarGridSpec(
        num_scalar_prefetch=1,
        grid=(n_slots // tg,),
        in_specs=[pl.BlockSpec(memory_space=pl.ANY)],
        out_specs=pl.BlockSpec((tg, d), lambda i, s: (i, 0)),
        scratch_shapes=[pltpu.VMEM((2, tg, d), src.dtype), pltpu.SemaphoreType.DMA((2,))],
    )
    return pl.pallas_call(
        functools.partial(_gather_kernel, tg=tg),
        grid_spec=grid_spec,
        out_shape=jax.ShapeDtypeStruct((n_slots, d), out_dtype),
        compiler_params=_params("arbitrary"),
        name="gather_rows",
    )(idx, src)


SCATTER_SLOTS = 3


def _scatter_add_kernel(idx_ref, ye_ref, h_in_ref, h_ref, buf, rsem, wsem, *, tg, tiles_per_expert):
    del h_in_ref
    i = pl.program_id(0)
    first = i % tiles_per_expert == 0
    last = i % tiles_per_expert == tiles_per_expert - 1

    def read(tile):
        slot = tile % SCATTER_SLOTS

        def start(r, c):
            _row_copy(h_ref, buf.at[slot], idx_ref[tile * tg + r], r, rsem.at[slot]).start()
            return c
        lax.fori_loop(0, tg, start, 0, unroll=8)

    def write(tile):
        slot = tile % SCATTER_SLOTS

        def start(r, c):
            _row_copy(buf.at[slot], h_ref, r, idx_ref[tile * tg + r], wsem.at[slot]).start()
            return c
        lax.fori_loop(0, tg, start, 0, unroll=8)

    def wait_write(tile):
        slot = tile % SCATTER_SLOTS
        _tile_wait(buf.at[slot], h_ref, wsem.at[slot], tg)

    @pl.when(first)
    def _():
        read(i)

    @pl.when(jnp.logical_not(last))
    def _():
        read(i + 1)

    slot = i % SCATTER_SLOTS
    _tile_wait(h_ref, buf.at[slot], rsem.at[slot], tg)
    buf[slot] = buf[slot] + ye_ref[...]
    write(i)

    @pl.when(jnp.logical_not(first))
    def _():
        wait_write(i - 1)

    @pl.when(last)
    def _():
        wait_write(i)


def _scatter_add(h, ye, idx, cap):
    n_slots, d = ye.shape
    tg = _tile(cap, SCATTER_ROWS)
    grid_spec = pltpu.PrefetchScalarGridSpec(
        num_scalar_prefetch=1,
        grid=(n_slots // tg,),
        in_specs=[pl.BlockSpec((tg, d), lambda i, s: (i, 0)),
                  pl.BlockSpec(memory_space=pl.ANY)],
        out_specs=pl.BlockSpec(memory_space=pl.ANY),
        scratch_shapes=[pltpu.VMEM((SCATTER_SLOTS, tg, d), F32),
                        pltpu.SemaphoreType.DMA((SCATTER_SLOTS,)),
                        pltpu.SemaphoreType.DMA((SCATTER_SLOTS,))],
    )
    return pl.pallas_call(
        functools.partial(_scatter_add_kernel, tg=tg, tiles_per_expert=cap // tg),
        grid_spec=grid_spec,
        out_shape=jax.ShapeDtypeStruct(h.shape, h.dtype),
        input_output_aliases={2: 0},
        compiler_params=_params("arbitrary"),
        name="scatter_add",
    )(idx, ye, h)


def _ffn_up_kernel(x_ref, w1_ref, w3_ref, o_ref):
    x = x_ref[...]
    a = jnp.dot(x, w1_ref[...].astype(BF16), preferred_element_type=F32)
    b = jnp.dot(x, w3_ref[...].astype(BF16), preferred_element_type=F32)
    o_ref[...] = (jax.nn.silu(a) * b).astype(o_ref.dtype)


def _ffn_up(xe, w1, w3, cap, tn=256):
    n_experts, d, f = w1.shape
    tn = _tile(f, tn)
    wspec = pl.BlockSpec((None, d, tn), lambda e, j: (e, 0, j))
    return pl.pallas_call(
        _ffn_up_kernel,
        grid=(n_experts, f // tn),
        in_specs=[pl.BlockSpec((cap, d), lambda e, j: (e, 0), pipeline_mode=pl.Buffered(1)),
                  wspec, wspec],
        out_specs=pl.BlockSpec((cap, tn), lambda e, j: (e, j)),
        out_shape=jax.ShapeDtypeStruct((n_experts * cap, f), BF16),
        compiler_params=_params("arbitrary", "arbitrary"),
        name="ffn_up",
    )(xe, w1, w3)


def _ffn_down_kernel(h_ref, w2_ref, gate_ref, o_ref):
    acc = jnp.dot(h_ref[...], w2_ref[...].astype(BF16), preferred_element_type=F32)
    o_ref[...] = acc * gate_ref[...]


def _ffn_down(hm, w2, gates, cap, tn=256):
    n_experts, f, d = w2.shape
    tn = _tile(d, tn)
    return pl.pallas_call(
        _ffn_down_kernel,
        grid=(n_experts, d // tn),
        in_specs=[pl.BlockSpec((cap, f), lambda e, j: (e, 0), pipeline_mode=pl.Buffered(1)),
                  pl.BlockSpec((None, f, tn), lambda e, j: (e, 0, j)),
                  pl.BlockSpec((cap, 1), lambda e, j: (e, 0), pipeline_mode=pl.Buffered(1))],
        out_specs=pl.BlockSpec((cap, tn), lambda e, j: (e, j)),
        out_shape=jax.ShapeDtypeStruct((n_experts * cap, d), F32),
        compiler_params=_params("arbitrary", "arbitrary"),
        name="ffn_down",
    )(hm, w2, gates)


def _prepare_weights(rg_wa, rg_ba, rg_wx, rg_bx, router_w):
    n_blocks, blk = rg_wa.shape[1], rg_wa.shape[2]
    wg = jnp.concatenate([rg_wa[0], rg_wa[1], rg_wx[0], rg_wx[1]], axis=-1).astype(BF16)
    bg = jnp.concatenate([rg_ba[0].reshape(n_blocks, blk), rg_ba[1].reshape(n_blocks, blk),
                          rg_bx[0].reshape(n_blocks, blk), rg_bx[1].reshape(n_blocks, blk)], axis=-1)
    n_experts = router_w.shape[1]
    rw_pad = jnp.pad(router_w, ((0, 0), (0, LANES - n_experts))).astype(BF16)
    return dict(wg=wg, bg=bg, rw_pad=rw_pad)


def _layer(x, layer, pw, norm_mix_g, w_in, lambda_q1, lambda_k1, lambda_q2, lambda_k2,
           attn_subln_g, conv_w, conv_b, rg_lambda, rec_norm_g, w_out, norm_ffn_g,
           exp_w1, exp_w3, exp_w2, n_experts, attn_width):
    batch, seq, d = x.shape
    n = batch * seq
    n_heads = attn_width // V_HEAD_DIM
    rec_width = conv_w.shape[-1]
    x2 = x.reshape(n, d)

    xn = _rmsnorm(x2, norm_mix_g, BF16)
    q_scale = QK_HALF_DIM ** -0.5 * LOG2E
    qkv_scale = jnp.concatenate([jnp.full((1, attn_width), q_scale, F32),
                                 jnp.ones((1, 2 * attn_width), F32)], axis=1)
    qkv = _matmul(xn, w_in, 0, 3 * attn_width, qkv_scale, BF16)
    xg = _matmul(xn, w_in, 3 * attn_width, 2 * rec_width, jnp.ones((1, 2 * rec_width), F32), F32)
    lam_init = 0.8 - 0.6 * math.exp(-0.3 * layer)
    slopes = 2.0 ** (-8.0 * jnp.arange(1, n_heads + 1, dtype=F32) / n_heads)
    attn = _diff_attention(qkv, batch, seq, n_heads, slopes, lambda_q1, lambda_k1, lambda_q2,
                           lambda_k2, attn_subln_g, lam_init)
    a_f, u_f, a_b, u_b = _rg_gates(xg, batch, seq, conv_w, conv_b, pw["wg"], pw["bg"], rg_lambda)
    h_f = _scan_fwd(a_f, u_f, batch, seq)
    rec = _scan_bwd_finish(a_b, u_b, h_f, xg, rec_norm_g, batch, seq)
    h = _outproj(attn, rec, w_out, x2)

    cap = (EC_CAPACITY_FACTOR * n) // n_experts
    hn, aff = _router(h, norm_ffn_g, pw["rw_pad"], n_experts)
    dest = _route(aff, cap)
    bits_t = lax.bitcast_convert_type(aff[:, :n_experts].T, jnp.int32)
    idx, gate_bits = _compact(dest[:, :n_experts].T.reshape(-1), bits_t.reshape(-1),
                              n_experts, n, cap)
    idx = idx[:, :cap].reshape(-1)
    gates = lax.bitcast_convert_type(gate_bits[:, :cap], F32).reshape(-1, 1)
    xe = _gather_rows(hn, idx, BF16)
    hm = _ffn_up(xe, exp_w1, exp_w3, cap)
    ye = _ffn_down(hm, exp_w2, gates, cap)
    return _scatter_add(h, ye, idx, cap)


def _trunk(x, pw, norm_mix_g, w_in, lambda_q1, lambda_k1, lambda_q2, lambda_k2, attn_subln_g,
           conv_w, conv_b, rg_lambda, rec_norm_g, w_out, norm_ffn_g, exp_w1, exp_w3, exp_w2,
           final_norm_g, n_experts, attn_width):
    batch, seq, d = x.shape
    h = x
    for l in range(norm_mix_g.shape[0]):
        h = _layer(h.reshape(batch, seq, d), l, pw[l], norm_mix_g[l], w_in[l], lambda_q1[l],
                   lambda_k1[l], lambda_q2[l], lambda_k2[l], attn_subln_g[l], conv_w[l], conv_b[l],
                   rg_lambda[l], rec_norm_g[l], w_out[l], norm_ffn_g[l], exp_w1[l], exp_w3[l],
                   exp_w2[l], n_experts, attn_width)
    return _rmsnorm(h.reshape(batch * seq, d), final_norm_g, F32).reshape(batch, seq, d)


def kernel(x_prompt, x_sample, norm_mix_g, w_in, lambda_q1, lambda_k1, lambda_q2, lambda_k2, attn_subln_g, conv_w, conv_b, rg_wa, rg_ba, rg_wx, rg_bx, rg_lambda, rec_norm_g, w_out, norm_ffn_g, router_w, exp_w1, exp_w3, exp_w2, final_norm_g):
    depth = w_in.shape[0]
    rec_width = conv_w.shape[-1]
    attn_width = (w_in.shape[-1] - 2 * rec_width) // 3
    n_experts = router_w.shape[-1]
    pw = [_prepare_weights(rg_wa[l], rg_ba[l], rg_wx[l], rg_bx[l], router_w[l]) for l in range(depth)]
    run = lambda x: _trunk(x, pw, norm_mix_g, w_in, lambda_q1, lambda_k1, lambda_q2, lambda_k2,
                           attn_subln_g, conv_w, conv_b, rg_lambda, rec_norm_g, w_out, norm_ffn_g,
                           exp_w1, exp_w3, exp_w2, final_norm_g, n_experts, attn_width)
    return (run(x_prompt), run(x_sample))
```

```python
import functools
import math

import jax
import jax.numpy as jnp
from jax import lax
from jax.experimental import pallas as pl
from jax.experimental.pallas import tpu as pltpu

V_HEAD_DIM = 128
QK_HALF_DIM = V_HEAD_DIM // 2
CONV_WIDTH = 4
RG_C = 8.0
EC_CAPACITY_FACTOR = 2
NORM_EPS = 1e-6
LANES = 128
SUBLANES = 8
SMEM_TILE_1D = 1024
VMEM_LIMIT_BYTES = 52 * 1024 * 1024

GATHER_ROWS = 512
SCATTER_ROWS = 256
ATTN_TQ = 512
ATTN_TK = 1024
LOG2E = math.log2(math.e)
SKIP_LOG2 = 50.0
FAST_LOG2 = 110.0
NORM_SLACK = 1.01

F32 = jnp.float32
BF16 = jnp.bfloat16


def _params(*sem):
    return pltpu.CompilerParams(dimension_semantics=sem, vmem_limit_bytes=VMEM_LIMIT_BYTES)


def _tile(n, want):
    t = min(n, want)
    while n % t:
        t //= 2
    return t


def _rms_kernel(x_ref, g_ref, o_ref):
    x = x_ref[...]
    inv = lax.rsqrt(jnp.mean(x * x, axis=-1, keepdims=True) + NORM_EPS)
    o_ref[...] = (x * inv * g_ref[...]).astype(o_ref.dtype)


def _rmsnorm(x, g, out_dtype):
    n, d = x.shape
    tm = _tile(n, 256)
    return pl.pallas_call(
        _rms_kernel,
        grid=(n // tm,),
        in_specs=[pl.BlockSpec((tm, d), lambda i: (i, 0)),
                  pl.BlockSpec((1, d), lambda i: (0, 0))],
        out_specs=pl.BlockSpec((tm, d), lambda i: (i, 0)),
        out_shape=jax.ShapeDtypeStruct((n, d), out_dtype),
        compiler_params=_params("arbitrary"),
        name="rmsnorm",
    )(x, g.reshape(1, d))


def _mm_kernel(x_ref, w_ref, s_ref, o_ref, wb_sc):
    @pl.when(pl.program_id(1) == 0)
    def _():
        wb_sc[...] = w_ref[...].astype(BF16)

    acc = jnp.dot(x_ref[...], wb_sc[...], preferred_element_type=F32)
    o_ref[...] = (acc * s_ref[...]).astype(o_ref.dtype)


def _matmul(x, w, col0, n, col_scale, out_dtype, tm=1024, tn=512):
    m, k = x.shape
    tm, tn = _tile(m, tm), _tile(math.gcd(n, col0), tn)
    jb = col0 // tn
    return pl.pallas_call(
        _mm_kernel,
        grid=(n // tn, m // tm),
        in_specs=[pl.BlockSpec((tm, k), lambda j, i: (i, 0)),
                  pl.BlockSpec((k, tn), lambda j, i: (0, jb + j)),
                  pl.BlockSpec((1, tn), lambda j, i: (0, j))],
        out_specs=pl.BlockSpec((tm, tn), lambda j, i: (i, j)),
        out_shape=jax.ShapeDtypeStruct((m, n), out_dtype),
        scratch_shapes=[pltpu.VMEM((k, tn), BF16)],
        compiler_params=_params("arbitrary", "arbitrary"),
        name="in_proj",
    )(x, w, col_scale)


def _outproj_kernel(a_ref, r_ref, wt_ref, wb_ref, x_ref, o_ref, wt_sc, wb_sc):
    @pl.when(pl.program_id(1) == 0)
    def _():
        wt_sc[...] = wt_ref[...].astype(BF16)
        wb_sc[...] = wb_ref[...].astype(BF16)

    acc = jnp.dot(a_ref[...], wt_sc[...], preferred_element_type=F32)
    acc = acc + jnp.dot(r_ref[...], wb_sc[...], preferred_element_type=F32)
    o_ref[...] = x_ref[...] + acc


def _outproj(attn, rec, w, x, tm=1024, tn=512):
    m, ka = attn.shape
    kr = rec.shape[1]
    assert ka == kr
    n = w.shape[1]
    tm, tn = _tile(m, tm), _tile(n, tn)
    return pl.pallas_call(
        _outproj_kernel,
        grid=(n // tn, m // tm),
        in_specs=[pl.BlockSpec((tm, ka), lambda j, i: (i, 0)),
                  pl.BlockSpec((tm, kr), lambda j, i: (i, 0)),
                  pl.BlockSpec((ka, tn), lambda j, i: (0, j)),
                  pl.BlockSpec((kr, tn), lambda j, i: (1, j)),
                  pl.BlockSpec((tm, tn), lambda j, i: (i, j))],
        out_specs=pl.BlockSpec((tm, tn), lambda j, i: (i, j)),
        out_shape=jax.ShapeDtypeStruct((m, n), F32),
        scratch_shapes=[pltpu.VMEM((ka, tn), BF16), pltpu.VMEM((kr, tn), BF16)],
        compiler_params=_params("arbitrary", "arbitrary"),
        name="out_proj",
    )(attn, rec, w, w, x)


def _qk_bound_kernel(q_ref, k_ref, g_ref, o_ref):
    @pl.when(pl.program_id(1) == 0)
    def _():
        o_ref[...] = jnp.zeros(o_ref.shape, F32)

    def group_max(x_ref):
        x = x_ref[...].astype(F32)
        sq = x * x
        hi = sq.astype(BF16)
        lo = (sq - hi.astype(F32)).astype(BF16)
        s = (jnp.dot(hi, g_ref[...], preferred_element_type=F32)
             + jnp.dot(lo, g_ref[...], preferred_element_type=F32))
        return jnp.max(s, axis=0, keepdims=True)

    o_ref[0:1, :] = jnp.maximum(o_ref[0:1, :], group_max(q_ref))
    o_ref[1:2, :] = jnp.maximum(o_ref[1:2, :], group_max(k_ref))


def _qk_bounds(qkv, batch, seq, attn_width):
    n_groups = attn_width // QK_HALF_DIM
    assert n_groups <= LANES
    tm = _tile(seq, 512)
    nt = seq // tm
    g = (jnp.arange(attn_width)[:, None] // QK_HALF_DIM == jnp.arange(LANES)[None, :]).astype(BF16)
    return pl.pallas_call(
        _qk_bound_kernel,
        grid=(batch, nt),
        in_specs=[pl.BlockSpec((tm, attn_width), lambda b, j: (b * nt + j, 0)),
                  pl.BlockSpec((tm, attn_width), lambda b, j: (b * nt + j, 1)),
                  pl.BlockSpec((attn_width, LANES), lambda b, j: (0, 0))],
        out_specs=pl.BlockSpec((None, 2, LANES), lambda b, j: (b, 0, 0)),
        out_shape=jax.ShapeDtypeStruct((batch, 2, LANES), F32),
        compiler_params=_params("arbitrary", "arbitrary"),
        name="qk_bounds",
    )(qkv, qkv, g)


def _attn_kernel(slopes_ref, band_ref, fast_ref, kmax_ref, q_ref, k_ref, v_ref, lq1_ref, lk1_ref,
                 lq2_ref, lk2_ref, g_ref, o_ref, m_sc, l_sc, acc_sc, lp_sc,
                 *, tq, tk, seq, n_heads, lam_init):
    b = pl.program_id(0)
    h = pl.program_id(1)
    bh = b * n_heads + h
    slope = slopes_ref[h]
    band = band_ref[bh]
    nk = seq // tk
    nch = tk // LANES
    q0 = pl.program_id(2) * tq
    kd = lax.div(q0, tk)
    lo = jnp.maximum(kd - band, 0)
    hi = jnp.minimum(kd + 1 + band, nk)

    q = q_ref[...]
    lane = lax.broadcasted_iota(jnp.int32, (tq, V_HEAD_DIM), 1)
    zero = jnp.zeros_like(q)
    qm = jnp.concatenate([jnp.where(lane < QK_HALF_DIM, q, zero),
                          jnp.where(lane >= QK_HALF_DIM, q, zero)], axis=0)
    d0 = (lax.broadcasted_iota(jnp.int32, (tq, tk), 0)
          - lax.broadcasted_iota(jnp.int32, (tq, tk), 1))

    def scores(k0):
        return lax.dot_general(qm, k_ref[pl.ds(k0, tk), :], (((1,), (1,)), ((), ())),
                               preferred_element_type=F32)

    def abs_bias(k0):
        bias = slope * jnp.abs(d0 + (q0 - k0)).astype(F32)
        return jnp.concatenate([bias, bias], axis=0)

    def finalize(l):
        lam = (jnp.exp(jnp.sum(lq1_ref[...] * lk1_ref[...], axis=1, keepdims=True))
               - jnp.exp(jnp.sum(lq2_ref[...] * lk2_ref[...], axis=1, keepdims=True))
               + lam_init)
        o = acc_sc[...] / l
        o = o[:tq] - lam * o[tq:]
        inv = lax.rsqrt(jnp.mean(o * o, axis=-1, keepdims=True) + NORM_EPS)
        o_ref[...] = ((o * inv * g_ref[...]) * (1.0 - lam_init)).astype(o_ref.dtype)

    @pl.when(fast_ref[bh] != 0)
    def _():
        qf = qm.astype(F32)
        nq = jnp.sqrt(jnp.sum(qf * qf, axis=1, keepdims=True))
        row = lax.broadcasted_iota(jnp.int32, (2 * tq, 1), 0)
        km = jnp.where(row < tq, kmax_ref[2 * bh], kmax_ref[2 * bh + 1])
        m_rep = jnp.broadcast_to(nq * km, (2 * tq, LANES))
        il = lax.broadcasted_iota(jnp.int32, (2 * tq, LANES), 0)
        il = slope * jnp.where(il >= tq, il - tq, il).astype(F32)
        jl = lax.broadcasted_iota(jnp.int32, (1, tk), 1).astype(F32)
        col_l = slope * (jl - (tk - 1))
        col_r = -slope * jl
        acc_sc[...] = jnp.zeros(acc_sc.shape, F32)
        lp_sc[...] = jnp.zeros(lp_sc.shape, F32)

        def tiled(x):
            return jnp.concatenate([x] * nch, axis=1)

        def weights(e, k0):
            p = jnp.exp2(e)
            lp = p[:, 0:LANES]
            for c in range(1, nch):
                lp = lp + p[:, c * LANES:(c + 1) * LANES]
            return lp, jnp.dot(p.astype(BF16), v_ref[pl.ds(k0, tk), :], preferred_element_type=F32)

        def accumulate(*parts):
            lp_sc[...] += functools.reduce(jnp.add, [lp for lp, _ in parts])
            acc_sc[...] += functools.reduce(jnp.add, [pv for _, pv in parts])

        def left(kj):
            k0 = pl.multiple_of(kj * tk, tk)
            mrow = (m_rep + il) - slope * (k0 + (tk - 1) - q0).astype(F32)
            return weights((scores(k0) + col_l) - tiled(mrow), k0)

        def right(kj):
            k0 = pl.multiple_of(kj * tk, tk)
            mrow = (m_rep - il) - slope * (q0 - k0).astype(F32)
            return weights((scores(k0) + col_r) - tiled(mrow), k0)

        def sweep(start, stop, one):
            count = stop - start

            def pair(t, c):
                kj = start + 2 * t
                accumulate(one(kj), one(kj + 1))
                return c

            lax.fori_loop(0, lax.shift_right_logical(count, 1), pair, 0)

            @pl.when(lax.bitwise_and(count, 1) == 1)
            def _():
                accumulate(one(stop - 1))

        k0d = pl.multiple_of(kd * tk, tk)
        accumulate(weights((scores(k0d) - abs_bias(k0d)) - tiled(m_rep), k0d))
        sweep(lo, kd, left)
        sweep(kd + 1, hi, right)
        finalize(jnp.sum(lp_sc[...], axis=1, keepdims=True))

    @pl.when(fast_ref[bh] == 0)
    def _():
        m_sc[...] = jnp.full(m_sc.shape, -jnp.inf, F32)
        l_sc[...] = jnp.zeros(l_sc.shape, F32)
        acc_sc[...] = jnp.zeros(acc_sc.shape, F32)

        def body(kj, c):
            k0 = pl.multiple_of(kj * tk, tk)
            s = scores(k0) - abs_bias(k0)
            m_prev = m_sc[...]
            m_new = jnp.maximum(m_prev, jnp.max(s, axis=1, keepdims=True))
            alpha = jnp.exp2(m_prev - m_new)
            p = jnp.exp2(s - m_new)
            l_sc[...] = alpha * l_sc[...] + jnp.sum(p, axis=1, keepdims=True)
            acc_sc[...] = alpha * acc_sc[...] + jnp.dot(p.astype(BF16), v_ref[pl.ds(k0, tk), :],
                                                         preferred_element_type=F32)
            m_sc[...] = m_new
            return c

        lax.fori_loop(lo, hi, body, 0)
        finalize(l_sc[...])


def _diff_attention(qkv, batch, seq, n_heads, slopes, lq1, lk1, lq2, lk2, subln_g, lam_init):
    n = batch * seq
    attn_width = n_heads * V_HEAD_DIM
    tq = _tile(seq, ATTN_TQ)
    tk = _tile(seq, ATTN_TK)
    assert tk % tq == 0 and tk % LANES == 0
    nq = seq // tq

    sq = _qk_bounds(qkv, batch, seq, attn_width)
    norms = jnp.sqrt(sq[:, :, :2 * n_heads]).reshape(batch, 2, n_heads, 2) * NORM_SLACK
    qmax, kmax = norms[:, 0], norms[:, 1]
    bmax = jnp.max(qmax * kmax, axis=-1)
    slopes2 = slopes * LOG2E
    band = jnp.floor((2.0 * bmax + SKIP_LOG2) / (slopes2[None, :] * tk)) + 1.0
    band = jnp.minimum(band, seq // tk).astype(jnp.int32).reshape(-1)
    fast = (2.0 * bmax <= FAST_LOG2).astype(jnp.int32).reshape(-1)

    kernel = functools.partial(_attn_kernel, tq=tq, tk=tk, seq=seq, n_heads=n_heads,
                               lam_init=lam_init)
    vec = lambda v: v.reshape(1, -1).astype(F32)
    small = lambda w: pl.BlockSpec((1, w), lambda b, h, i, *_: (0, 0))
    grid_spec = pltpu.PrefetchScalarGridSpec(
        num_scalar_prefetch=4,
        grid=(batch, n_heads, nq),
        in_specs=[pl.BlockSpec((tq, V_HEAD_DIM), lambda b, h, i, *_: (b * nq + i, h)),
                  pl.BlockSpec((seq, V_HEAD_DIM), lambda b, h, i, *_: (b, n_heads + h)),
                  pl.BlockSpec((seq, V_HEAD_DIM), lambda b, h, i, *_: (b, 2 * n_heads + h)),
                  small(QK_HALF_DIM), small(QK_HALF_DIM), small(QK_HALF_DIM), small(QK_HALF_DIM),
                  small(V_HEAD_DIM)],
        out_specs=pl.BlockSpec((tq, V_HEAD_DIM), lambda b, h, i, *_: (b * nq + i, h)),
        scratch_shapes=[pltpu.VMEM((2 * tq, 1), F32), pltpu.VMEM((2 * tq, 1), F32),
                        pltpu.VMEM((2 * tq, V_HEAD_DIM), F32), pltpu.VMEM((2 * tq, LANES), F32)],
    )
    return pl.pallas_call(
        kernel,
        grid_spec=grid_spec,
        out_shape=jax.ShapeDtypeStruct((n, attn_width), BF16),
        compiler_params=_params("arbitrary", "arbitrary", "arbitrary"),
        name="diff_attn",
    )(slopes2, band, fast, kmax.reshape(-1), qkv, qkv, qkv,
      vec(lq1), vec(lk1), vec(lq2), vec(lk2), vec(subln_g))


def _gates_kernel(cur_ref, prev_ref, next_ref, cw_ref, cb_ref, wg_ref, bg_ref, lam_ref,
                  af_ref, uf_ref, ab_ref, ub_ref, ext_sc, *, t, n_blocks, blk):
    j = pl.program_id(1)
    nt = pl.num_programs(1)
    halo = SUBLANES
    ext_sc[0:halo, :] = jnp.where(j > 0, prev_ref[...], 0.0)
    ext_sc[halo:halo + t, :] = cur_ref[...]
    ext_sc[halo + t:2 * halo + t, :] = jnp.where(j < nt - 1, next_ref[...], 0.0)
    left = CONV_WIDTH // 2
    y = cb_ref[...]
    for c in range(CONV_WIDTH):
        y = y + ext_sc[halo - left + c:halo - left + c + t, :] * cw_ref[c:c + 1, :]
    nl = -lam_ref[...]
    sp = jnp.maximum(nl, 0.0) + jnp.log1p(jnp.exp(-jnp.abs(nl)))
    outs = ((af_ref, uf_ref), (ab_ref, ub_ref))
    for g in range(n_blocks):
        cs = slice(g * blk, (g + 1) * blk)
        xc = y[:, cs]
        pre = jnp.dot(xc.astype(BF16), wg_ref[g], preferred_element_type=F32) + bg_ref[g:g + 1, :]
        for d in range(2):
            r = jax.nn.sigmoid(pre[:, d * blk:(d + 1) * blk])
            i = jax.nn.sigmoid(pre[:, (2 + d) * blk:(3 + d) * blk])
            log_a = (-RG_C * r) * sp[d:d + 1, cs]
            a = jnp.exp(log_a)
            u = jnp.sqrt(1.0 - jnp.exp(2.0 * log_a)) * (i * xc)
            outs[d][0][:, cs] = a
            outs[d][1][:, cs] = u


def _rg_gates(xg, batch, seq, conv_w, conv_b, wg, bg, lam):
    n = batch * seq
    c = xg.shape[1] // 2
    n_blocks, blk = wg.shape[0], wg.shape[1]
    t = _tile(seq, 256)
    nt = seq // t
    hb = t // SUBLANES
    last = n // SUBLANES - 1
    kernel = functools.partial(_gates_kernel, t=t, n_blocks=n_blocks, blk=blk)
    full = lambda shape: pl.BlockSpec(shape, lambda b, j: (0,) * len(shape))
    out_spec = pl.BlockSpec((t, c), lambda b, j: (b * nt + j, 0))
    return pl.pallas_call(
        kernel,
        grid=(batch, nt),
        in_specs=[pl.BlockSpec((t, c), lambda b, j: (b * nt + j, 0)),
                  pl.BlockSpec((SUBLANES, c), lambda b, j: (jnp.maximum((b * nt + j) * hb - 1, 0), 0)),
                  pl.BlockSpec((SUBLANES, c), lambda b, j: (jnp.minimum((b * nt + j + 1) * hb, last), 0)),
                  full((CONV_WIDTH, c)), full((1, c)),
                  full(wg.shape), full(bg.shape), full((2, c))],
        out_specs=[out_spec] * 4,
        out_shape=[jax.ShapeDtypeStruct((n, c), F32)] * 4,
        scratch_shapes=[pltpu.VMEM((t + 2 * SUBLANES, c), F32)],
        compiler_params=_params("arbitrary", "arbitrary"),
        name="rg_gates",
    )(xg, xg, xg, conv_w, conv_b.reshape(1, c), wg, bg, lam)


def _scan_fwd_kernel(a_ref, u_ref, h_ref, carry_sc, *, t):
    @pl.when(pl.program_id(1) == 0)
    def _():
        carry_sc[...] = jnp.zeros(carry_sc.shape, F32)

    def body(i, h):
        h = a_ref[pl.ds(i, 1), :] * h + u_ref[pl.ds(i, 1), :]
        h_ref[pl.ds(i, 1), :] = h
        return h

    carry_sc[...] = lax.fori_loop(0, t, body, carry_sc[...], unroll=8)


def _scan_fwd(a, u, batch, seq):
    n, c = a.shape
    t = _tile(seq, 256)
    nt = seq // t
    spec = pl.BlockSpec((t, c), lambda b, j: (b * nt + j, 0))
    return pl.pallas_call(
        functools.partial(_scan_fwd_kernel, t=t),
        grid=(batch, nt),
        in_specs=[spec, spec],
        out_specs=spec,
        out_shape=jax.ShapeDtypeStruct((n, c), F32),
        scratch_shapes=[pltpu.VMEM((1, c), F32)],
        compiler_params=_params("arbitrary", "arbitrary"),
        name="scan_fwd",
    )(a, u)


def _scan_bwd_kernel(a_ref, u_ref, hf_ref, gr_ref, g_ref, o_ref, carry_sc, hb_sc, *, t):
    @pl.when(pl.program_id(1) == 0)
    def _():
        carry_sc[...] = jnp.zeros(carry_sc.shape, F32)

    def body(i, h):
        r = t - 1 - i
        h = a_ref[pl.ds(r, 1), :] * h + u_ref[pl.ds(r, 1), :]
        hb_sc[pl.ds(r, 1), :] = h
        return h

    carry_sc[...] = lax.fori_loop(0, t, body, carry_sc[...], unroll=8)
    rec = (hf_ref[...] + hb_sc[...]) * jax.nn.gelu(gr_ref[...], approximate=True)
    inv = lax.rsqrt(jnp.mean(rec * rec, axis=-1, keepdims=True) + NORM_EPS)
    o_ref[...] = (rec * inv * g_ref[...]).astype(o_ref.dtype)


def _scan_bwd_finish(a, u, hf, xg, norm_g, batch, seq):
    n, c = a.shape
    t = _tile(seq, 256)
    nt = seq // t
    rev = lambda b, j: (b * nt + (nt - 1 - j), 0)
    spec = pl.BlockSpec((t, c), rev)
    return pl.pallas_call(
        functools.partial(_scan_bwd_kernel, t=t),
        grid=(batch, nt),
        in_specs=[spec, spec, spec,
                  pl.BlockSpec((t, c), lambda b, j: (b * nt + (nt - 1 - j), 1)),
                  pl.BlockSpec((1, c), lambda b, j: (0, 0))],
        out_specs=spec,
        out_shape=jax.ShapeDtypeStruct((n, c), BF16),
        scratch_shapes=[pltpu.VMEM((1, c), F32), pltpu.VMEM((t, c), F32)],
        compiler_params=_params("arbitrary", "arbitrary"),
        name="scan_bwd",
    )(a, u, hf, xg, norm_g.reshape(1, c))


def _router_kernel(h_ref, g_ref, rw_ref, hn_ref, aff_ref, *, n_experts):
    x = h_ref[...]
    inv = lax.rsqrt(jnp.mean(x * x, axis=-1, keepdims=True) + NORM_EPS)
    hn = x * inv * g_ref[...]
    hn_ref[...] = hn
    logits = jnp.dot(hn.astype(BF16), rw_ref[...], preferred_element_type=F32)
    lane = lax.broadcasted_iota(jnp.int32, logits.shape, 1)
    logits = jnp.where(lane < n_experts, logits, -jnp.inf)
    e = jnp.exp(logits - jnp.max(logits, axis=-1, keepdims=True))
    aff_ref[...] = e / jnp.sum(e, axis=-1, keepdims=True)


def _router(h, g, rw_pad, n_experts):
    n, d = h.shape
    tm = _tile(n, 256)
    return pl.pallas_call(
        functools.partial(_router_kernel, n_experts=n_experts),
        grid=(n // tm,),
        in_specs=[pl.BlockSpec((tm, d), lambda i: (i, 0)),
                  pl.BlockSpec((1, d), lambda i: (0, 0)),
                  pl.BlockSpec((d, LANES), lambda i: (0, 0))],
        out_specs=[pl.BlockSpec((tm, d), lambda i: (i, 0)),
                   pl.BlockSpec((tm, LANES), lambda i: (i, 0))],
        out_shape=[jax.ShapeDtypeStruct((n, d), F32), jax.ShapeDtypeStruct((n, LANES), F32)],
        compiler_params=_params("arbitrary"),
        name="router",
    )(h, g.reshape(1, d), rw_pad)


def _route_kernel(aff_ref, dest_ref, *, cap, n_tok, chunk):
    n_chunks = n_tok // chunk

    def bits_of(c):
        return lax.bitcast_convert_type(aff_ref[pl.ds(pl.multiple_of(c * chunk, chunk), chunk), :],
                                        jnp.int32)

    def count(pred):
        def body(c, acc):
            return acc + jnp.sum(jnp.where(pred(bits_of(c)), 1.0, 0.0), axis=0, keepdims=True)
        return lax.fori_loop(0, n_chunks, body, jnp.zeros((1, LANES), F32))

    def search(i, thr):
        cand = thr | lax.shift_left(jnp.int32(1), 30 - i)
        return jnp.where(count(lambda v: v >= cand) >= cap, cand, thr)

    thr = lax.fori_loop(0, 31, search, jnp.zeros((1, LANES), jnp.int32))
    need = cap - count(lambda v: v > thr)

    ltri = jnp.where(lax.broadcasted_iota(jnp.int32, (chunk, chunk), 0)
                     > lax.broadcasted_iota(jnp.int32, (chunk, chunk), 1), 1.0, 0.0).astype(BF16)

    def place(c, carry):
        ties_before, taken_before = carry
        v = bits_of(c)
        tie = v == thr
        tie_f = jnp.where(tie, 1.0, 0.0)
        tie_rank = jnp.dot(ltri, tie_f.astype(BF16), preferred_element_type=F32) + ties_before
        take = jnp.logical_or(v > thr, jnp.logical_and(tie, tie_rank < need))
        take_f = jnp.where(take, 1.0, 0.0)
        pos = jnp.dot(ltri, take_f.astype(BF16), preferred_element_type=F32) + taken_before
        dest_ref[pl.ds(pl.multiple_of(c * chunk, chunk), chunk), :] = jnp.where(
            take, pos.astype(jnp.int32), cap)
        return (ties_before + jnp.sum(tie_f, axis=0, keepdims=True),
                taken_before + jnp.sum(take_f, axis=0, keepdims=True))

    zero = jnp.zeros((1, LANES), F32)
    lax.fori_loop(0, n_chunks, place, (zero, zero))


def _route(aff_pad, cap):
    n_tok = aff_pad.shape[0]
    chunk = _tile(n_tok, LANES)
    return pl.pallas_call(
        functools.partial(_route_kernel, cap=cap, n_tok=n_tok, chunk=chunk),
        out_shape=jax.ShapeDtypeStruct((n_tok, LANES), jnp.int32),
        compiler_params=pltpu.CompilerParams(vmem_limit_bytes=VMEM_LIMIT_BYTES),
        name="route",
    )(aff_pad)


def _compact_kernel(dest_ref, bits_ref, idx_ref, gate_ref, *, n_tok, cap):
    def clear(i, c):
        idx_ref[i] = 0
        gate_ref[i] = 0
        return c

    lax.fori_loop(cap, idx_ref.shape[0], clear, 0)

    def body(n, c):
        idx_ref[dest_ref[n]] = n
        return c

    lax.fori_loop(0, n_tok, body, 0, unroll=16)

    def gate(s, c):
        gate_ref[s] = bits_ref[idx_ref[s]]
        return c

    lax.fori_loop(0, cap, gate, 0, unroll=16)


def _compact(dest_flat, bits_flat, n_experts, n_tok, cap):
    width = cap + SMEM_TILE_1D
    in_spec = pl.BlockSpec((n_tok,), lambda e: (e,), memory_space=pltpu.SMEM)
    out_spec = pl.BlockSpec((width,), lambda e: (e,), memory_space=pltpu.SMEM)
    out = jax.ShapeDtypeStruct((n_experts * width,), jnp.int32)
    idx, gate = pl.pallas_call(
        functools.partial(_compact_kernel, n_tok=n_tok, cap=cap),
        grid=(n_experts,),
        in_specs=[in_spec, in_spec],
        out_specs=[out_spec, out_spec],
        out_shape=[out, out],
        compiler_params=_params("arbitrary"),
        name="compact",
    )(dest_flat, bits_flat)
    return idx.reshape(n_experts, width), gate.reshape(n_experts, width)


def _row_copy(src, dst, src_row, dst_row, sem):
    return pltpu.make_async_copy(src.at[pl.ds(src_row, 1), :], dst.at[pl.ds(dst_row, 1), :], sem)


def _tile_wait(src, dst, sem, rows):
    pltpu.make_async_copy(src.at[pl.ds(0, rows), :], dst.at[pl.ds(0, rows), :], sem).wait()


def _gather_kernel(idx_ref, src_ref, o_ref, buf, sem, *, tg):
    i = pl.program_id(0)
    n = pl.num_programs(0)

    def issue(tile, slot):
        def start(r, c):
            _row_copy(src_ref, buf.at[slot], idx_ref[tile * tg + r], r, sem.at[slot]).start()
            return c
        lax.fori_loop(0, tg, start, 0, unroll=8)

    @pl.when(i == 0)
    def _():
        issue(0, 0)

    @pl.when(i + 1 < n)
    def _():
        issue(i + 1, (i + 1) % 2)

    slot = i % 2
    _tile_wait(src_ref, buf.at[slot], sem.at[slot], tg)
    o_ref[...] = buf[slot].astype(o_ref.dtype)


def _gather_rows(src, idx, out_dtype):
    n_slots = idx.shape[0]
    d = src.shape[1]
    tg = _tile(n_slots, GATHER_ROWS)
    grid_spec = pltpu.PrefetchScalarGridSpec(
        num_scalar_prefetch=1,
        grid=(n_slots // tg,),
        in_specs=[pl.BlockSpec(memory_space=pl.ANY)],
        out_specs=pl.BlockSpec((tg, d), lambda i, s: (i, 0)),
        scratch_shapes=[pltpu.VMEM((2, tg, d), src.dtype), pltpu.SemaphoreType.DMA((2,))],
    )
    return pl.pallas_call(
        functools.partial(_gather_kernel, tg=tg),
        grid_spec=grid_spec,
        out_shape=jax.ShapeDtypeStruct((n_slots, d), out_dtype),
        compiler_params=_params("arbitrary"),
        name="gather_rows",
    )(idx, src)


SCATTER_SLOTS = 3


def _scatter_add_kernel(idx_ref, ye_ref, h_in_ref, h_ref, buf, rsem, wsem, *, tg, tiles_per_expert):
    del h_in_ref
    i = pl.program_id(0)
    first = i % tiles_per_expert == 0
    last = i % tiles_per_expert == tiles_per_expert - 1

    def read(tile):
        slot = tile % SCATTER_SLOTS

        def start(r, c):
            _row_copy(h_ref, buf.at[slot], idx_ref[tile * tg + r], r, rsem.at[slot]).start()
            return c
        lax.fori_loop(0, tg, start, 0, unroll=8)

    def write(tile):
        slot = tile % SCATTER_SLOTS

        def start(r, c):
            _row_copy(buf.at[slot], h_ref, r, idx_ref[tile * tg + r], wsem.at[slot]).start()
            return c
        lax.fori_loop(0, tg, start, 0, unroll=8)

    def wait_write(tile):
        slot = tile % SCATTER_SLOTS
        _tile_wait(buf.at[slot], h_ref, wsem.at[slot], tg)

    @pl.when(first)
    def _():
        read(i)

    @pl.when(jnp.logical_not(last))
    def _():
        read(i + 1)

    slot = i % SCATTER_SLOTS
    _tile_wait(h_ref, buf.at[slot], rsem.at[slot], tg)
    buf[slot] = buf[slot] + ye_ref[...]
    write(i)

    @pl.when(jnp.logical_not(first))
    def _():
        wait_write(i - 1)

    @pl.when(last)
    def _():
        wait_write(i)


def _scatter_add(h, ye, idx, cap):
    n_slots, d = ye.shape
    tg = _tile(cap, SCATTER_ROWS)
    grid_spec = pltpu.PrefetchScalarGridSpec(
        num_scalar_prefetch=1,
        grid=(n_slots // tg,),
        in_specs=[pl.BlockSpec((tg, d), lambda i, s: (i, 0)),
                  pl.BlockSpec(memory_space=pl.ANY)],
        out_specs=pl.BlockSpec(memory_space=pl.ANY),
        scratch_shapes=[pltpu.VMEM((SCATTER_SLOTS, tg, d), F32),
                        pltpu.SemaphoreType.DMA((SCATTER_SLOTS,)),
                        pltpu.SemaphoreType.DMA((SCATTER_SLOTS,))],
    )
    return pl.pallas_call(
        functools.partial(_scatter_add_kernel, tg=tg, tiles_per_expert=cap // tg),
        grid_spec=grid_spec,
        out_shape=jax.ShapeDtypeStruct(h.shape, h.dtype),
        input_output_aliases={2: 0},
        compiler_params=_params("arbitrary"),
        name="scatter_add",
    )(idx, ye, h)


def _ffn_up_kernel(x_ref, w1_ref, w3_ref, o_ref):
    x = x_ref[...]
    a = jnp.dot(x, w1_ref[...].astype(BF16), preferred_element_type=F32)
    b = jnp.dot(x, w3_ref[...].astype(BF16), preferred_element_type=F32)
    o_ref[...] = (jax.nn.silu(a) * b).astype(o_ref.dtype)


def _ffn_up(xe, w1, w3, cap, tn=256):
    n_experts, d, f = w1.shape
    tn = _tile(f, tn)
    wspec = pl.BlockSpec((None, d, tn), lambda e, j: (e, 0, j))
    return pl.pallas_call(
        _ffn_up_kernel,
        grid=(n_experts, f // tn),
        in_specs=[pl.BlockSpec((cap, d), lambda e, j: (e, 0), pipeline_mode=pl.Buffered(1)),
                  wspec, wspec],
        out_specs=pl.BlockSpec((cap, tn), lambda e, j: (e, j)),
        out_shape=jax.ShapeDtypeStruct((n_experts * cap, f), BF16),
        compiler_params=_params("arbitrary", "arbitrary"),
        name="ffn_up",
    )(xe, w1, w3)


def _ffn_down_kernel(h_ref, w2_ref, gate_ref, o_ref):
    acc = jnp.dot(h_ref[...], w2_ref[...].astype(BF16), preferred_element_type=F32)
    o_ref[...] = acc * gate_ref[...]


def _ffn_down(hm, w2, gates, cap, tn=256):
    n_experts, f, d = w2.shape
    tn = _tile(d, tn)
    return pl.pallas_call(
        _ffn_down_kernel,
        grid=(n_experts, d // tn),
        in_specs=[pl.BlockSpec((cap, f), lambda e, j: (e, 0), pipeline_mode=pl.Buffered(1)),
                  pl.BlockSpec((None, f, tn), lambda e, j: (e, 0, j)),
                  pl.BlockSpec((cap, 1), lambda e, j: (e, 0), pipeline_mode=pl.Buffered(1))],
        out_specs=pl.BlockSpec((cap, tn), lambda e, j: (e, j)),
        out_shape=jax.ShapeDtypeStruct((n_experts * cap, d), F32),
        compiler_params=_params("arbitrary", "arbitrary"),
        name="ffn_down",
    )(hm, w2, gates)


def _prepare_weights(rg_wa, rg_ba, rg_wx, rg_bx, router_w):
    n_blocks, blk = rg_wa.shape[1], rg_wa.shape[2]
    wg = jnp.concatenate([rg_wa[0], rg_wa[1], rg_wx[0], rg_wx[1]], axis=-1).astype(BF16)
    bg = jnp.concatenate([rg_ba[0].reshape(n_blocks, blk), rg_ba[1].reshape(n_blocks, blk),
                          rg_bx[0].reshape(n_blocks, blk), rg_bx[1].reshape(n_blocks, blk)], axis=-1)
    n_experts = router_w.shape[1]
    rw_pad = jnp.pad(router_w, ((0, 0), (0, LANES - n_experts))).astype(BF16)
    return dict(wg=wg, bg=bg, rw_pad=rw_pad)


def _layer(x, layer, pw, norm_mix_g, w_in, lambda_q1, lambda_k1, lambda_q2, lambda_k2,
           attn_subln_g, conv_w, conv_b, rg_lambda, rec_norm_g, w_out, norm_ffn_g,
           exp_w1, exp_w3, exp_w2, n_experts, attn_width):
    batch, seq, d = x.shape
    n = batch * seq
    n_heads = attn_width // V_HEAD_DIM
    rec_width = conv_w.shape[-1]
    x2 = x.reshape(n, d)

    xn = _rmsnorm(x2, norm_mix_g, BF16)
    q_scale = QK_HALF_DIM ** -0.5 * LOG2E
    qkv_scale = jnp.concatenate([jnp.full((1, attn_width), q_scale, F32),
                                 jnp.ones((1, 2 * attn_width), F32)], axis=1)
    qkv = _matmul(xn, w_in, 0, 3 * attn_width, qkv_scale, BF16)
    xg = _matmul(xn, w_in, 3 * attn_width, 2 * rec_width, jnp.ones((1, 2 * rec_width), F32), F32)
    lam_init = 0.8 - 0.6 * math.exp(-0.3 * layer)
    slopes = 2.0 ** (-8.0 * jnp.arange(1, n_heads + 1, dtype=F32) / n_heads)
    attn = _diff_attention(qkv, batch, seq, n_heads, slopes, lambda_q1, lambda_k1, lambda_q2,
                           lambda_k2, attn_subln_g, lam_init)
    a_f, u_f, a_b, u_b = _rg_gates(xg, batch, seq, conv_w, conv_b, pw["wg"], pw["bg"], rg_lambda)
    h_f = _scan_fwd(a_f, u_f, batch, seq)
    rec = _scan_bwd_finish(a_b, u_b, h_f, xg, rec_norm_g, batch, seq)
    h = _outproj(attn, rec, w_out, x2)

    cap = (EC_CAPACITY_FACTOR * n) // n_experts
    hn, aff = _router(h, norm_ffn_g, pw["rw_pad"], n_experts)
    dest = _route(aff, cap)
    bits_t = lax.bitcast_convert_type(aff[:, :n_experts].T, jnp.int32)
    idx, gate_bits = _compact(dest[:, :n_experts].T.reshape(-1), bits_t.reshape(-1),
                              n_experts, n, cap)
    idx = idx[:, :cap].reshape(-1)
    gates = lax.bitcast_convert_type(gate_bits[:, :cap], F32).reshape(-1, 1)
    xe = _gather_rows(hn, idx, BF16)
    hm = _ffn_up(xe, exp_w1, exp_w3, cap)
    ye = _ffn_down(hm, exp_w2, gates, cap)
    return _scatter_add(h, ye, idx, cap)


def _trunk(x, pw, norm_mix_g, w_in, lambda_q1, lambda_k1, lambda_q2, lambda_k2, attn_subln_g,
           conv_w, conv_b, rg_lambda, rec_norm_g, w_out, norm_ffn_g, exp_w1, exp_w3, exp_w2,
           final_norm_g, n_experts, attn_width):
    batch, seq, d = x.shape
    h = x
    for l in range(norm_mix_g.shape[0]):
        h = _layer(h.reshape(batch, seq, d), l, pw[l], norm_mix_g[l], w_in[l], lambda_q1[l],
                   lambda_k1[l], lambda_q2[l], lambda_k2[l], attn_subln_g[l], conv_w[l], conv_b[l],
                   rg_lambda[l], rec_norm_g[l], w_out[l], norm_ffn_g[l], exp_w1[l], exp_w3[l],
                   exp_w2[l], n_experts, attn_width)
    return _rmsnorm(h.reshape(batch * seq, d), final_norm_g, F32).reshape(batch, seq, d)


def kernel(x_prompt, x_sample, norm_mix_g, w_in, lambda_q1, lambda_k1, lambda_q2, lambda_k2, attn_subln_g, conv_w, conv_b, rg_wa, rg_ba, rg_wx, rg_bx, rg_lambda, rec_norm_g, w_out, norm_ffn_g, router_w, exp_w1, exp_w3, exp_w2, final_norm_g):
    depth = w_in.shape[0]
    rec_width = conv_w.shape[-1]
    attn_width = (w_in.shape[-1] - 2 * rec_width) // 3
    n_experts = router_w.shape[-1]
    pw = [_prepare_weights(rg_wa[l], rg_ba[l], rg_wx[l], rg_bx[l], router_w[l]) for l in range(depth)]
    run = lambda x: _trunk(x, pw, norm_mix_g, w_in, lambda_q1, lambda_k1, lambda_q2, lambda_k2,
                           attn_subln_g, conv_w, conv_b, rg_lambda, rec_norm_g, w_out, norm_ffn_g,
                           exp_w1, exp_w3, exp_w2, final_norm_g, n_experts, attn_width)
    return (run(x_prompt), run(x_sample))
```

```python
import functools
import math

import jax
import jax.numpy as jnp
from jax import lax
from jax.experimental import pallas as pl
from jax.experimental.pallas import tpu as pltpu

V_HEAD_DIM = 128
QK_HALF_DIM = V_HEAD_DIM // 2
CONV_WIDTH = 4
RG_C = 8.0
EC_CAPACITY_FACTOR = 2
NORM_EPS = 1e-6
LANES = 128
SUBLANES = 8
SMEM_TILE_1D = 1024
VMEM_LIMIT_BYTES = 52 * 1024 * 1024

GATHER_ROWS = 512
SCATTER_ROWS = 256
ATTN_TQ = 512
ATTN_TK = 1024
LOG2E = math.log2(math.e)
SKIP_LOG2 = 40.0
FAST_LOG2 = 110.0
NORM_SLACK = 1.01

F32 = jnp.float32
BF16 = jnp.bfloat16


def _params(*sem):
    return pltpu.CompilerParams(dimension_semantics=sem, vmem_limit_bytes=VMEM_LIMIT_BYTES)


def _tile(n, want):
    t = min(n, want)
    while n % t:
        t //= 2
    return t


def _rms_kernel(x_ref, g_ref, o_ref):
    x = x_ref[...]
    inv = lax.rsqrt(jnp.mean(x * x, axis=-1, keepdims=True) + NORM_EPS)
    o_ref[...] = (x * inv * g_ref[...]).astype(o_ref.dtype)


def _rmsnorm(x, g, out_dtype):
    n, d = x.shape
    tm = _tile(n, 256)
    return pl.pallas_call(
        _rms_kernel,
        grid=(n // tm,),
        in_specs=[pl.BlockSpec((tm, d), lambda i: (i, 0)),
                  pl.BlockSpec((1, d), lambda i: (0, 0))],
        out_specs=pl.BlockSpec((tm, d), lambda i: (i, 0)),
        out_shape=jax.ShapeDtypeStruct((n, d), out_dtype),
        compiler_params=_params("arbitrary"),
        name="rmsnorm",
    )(x, g.reshape(1, d))


def _mm_kernel(x_ref, w_ref, s_ref, o_ref, wb_sc):
    @pl.when(pl.program_id(1) == 0)
    def _():
        wb_sc[...] = w_ref[...].astype(BF16)

    acc = jnp.dot(x_ref[...], wb_sc[...], preferred_element_type=F32)
    o_ref[...] = (acc * s_ref[...]).astype(o_ref.dtype)


def _matmul(x, w, col0, n, col_scale, out_dtype, tm=1024, tn=512):
    m, k = x.shape
    tm, tn = _tile(m, tm), _tile(math.gcd(n, col0), tn)
    jb = col0 // tn
    return pl.pallas_call(
        _mm_kernel,
        grid=(n // tn, m // tm),
        in_specs=[pl.BlockSpec((tm, k), lambda j, i: (i, 0)),
                  pl.BlockSpec((k, tn), lambda j, i: (0, jb + j)),
                  pl.BlockSpec((1, tn), lambda j, i: (0, j))],
        out_specs=pl.BlockSpec((tm, tn), lambda j, i: (i, j)),
        out_shape=jax.ShapeDtypeStruct((m, n), out_dtype),
        scratch_shapes=[pltpu.VMEM((k, tn), BF16)],
        compiler_params=_params("arbitrary", "arbitrary"),
        name="in_proj",
    )(x, w, col_scale)


def _outproj_kernel(a_ref, r_ref, wt_ref, wb_ref, x_ref, o_ref, wt_sc, wb_sc):
    @pl.when(pl.program_id(1) == 0)
    def _():
        wt_sc[...] = wt_ref[...].astype(BF16)
        wb_sc[...] = wb_ref[...].astype(BF16)

    acc = jnp.dot(a_ref[...], wt_sc[...], preferred_element_type=F32)
    acc = acc + jnp.dot(r_ref[...], wb_sc[...], preferred_element_type=F32)
    o_ref[...] = x_ref[...] + acc


def _outproj(attn, rec, w, x, tm=1024, tn=512):
    m, ka = attn.shape
    kr = rec.shape[1]
    assert ka == kr
    n = w.shape[1]
    tm, tn = _tile(m, tm), _tile(n, tn)
    return pl.pallas_call(
        _outproj_kernel,
        grid=(n // tn, m // tm),
        in_specs=[pl.BlockSpec((tm, ka), lambda j, i: (i, 0)),
                  pl.BlockSpec((tm, kr), lambda j, i: (i, 0)),
                  pl.BlockSpec((ka, tn), lambda j, i: (0, j)),
                  pl.BlockSpec((kr, tn), lambda j, i: (1, j)),
                  pl.BlockSpec((tm, tn), lambda j, i: (i, j))],
        out_specs=pl.BlockSpec((tm, tn), lambda j, i: (i, j)),
        out_shape=jax.ShapeDtypeStruct((m, n), F32),
        scratch_shapes=[pltpu.VMEM((ka, tn), BF16), pltpu.VMEM((kr, tn), BF16)],
        compiler_params=_params("arbitrary", "arbitrary"),
        name="out_proj",
    )(attn, rec, w, w, x)


def _qk_bound_kernel(q_ref, k_ref, g_ref, o_ref):
    @pl.when(pl.program_id(1) == 0)
    def _():
        o_ref[...] = jnp.zeros(o_ref.shape, F32)

    def group_max(x_ref):
        x = x_ref[...].astype(F32)
        sq = x * x
        hi = sq.astype(BF16)
        lo = (sq - hi.astype(F32)).astype(BF16)
        s = (jnp.dot(hi, g_ref[...], preferred_element_type=F32)
             + jnp.dot(lo, g_ref[...], preferred_element_type=F32))
        return jnp.max(s, axis=0, keepdims=True)

    o_ref[0:1, :] = jnp.maximum(o_ref[0:1, :], group_max(q_ref))
    o_ref[1:2, :] = jnp.maximum(o_ref[1:2, :], group_max(k_ref))


def _qk_bounds(qkv, batch, seq, attn_width):
    n_groups = attn_width // QK_HALF_DIM
    assert n_groups <= LANES
    tm = _tile(seq, 512)
    nt = seq // tm
    g = (jnp.arange(attn_width)[:, None] // QK_HALF_DIM == jnp.arange(LANES)[None, :]).astype(BF16)
    return pl.pallas_call(
        _qk_bound_kernel,
        grid=(batch, nt),
        in_specs=[pl.BlockSpec((tm, attn_width), lambda b, j: (b * nt + j, 0)),
                  pl.BlockSpec((tm, attn_width), lambda b, j: (b * nt + j, 1)),
                  pl.BlockSpec((attn_width, LANES), lambda b, j: (0, 0))],
        out_specs=pl.BlockSpec((None, 2, LANES), lambda b, j: (b, 0, 0)),
        out_shape=jax.ShapeDtypeStruct((batch, 2, LANES), F32),
        compiler_params=_params("arbitrary", "arbitrary"),
        name="qk_bounds",
    )(qkv, qkv, g)


def _attn_kernel(slopes_ref, band_ref, fast_ref, kmax_ref, q_ref, k_ref, v_ref, lq1_ref, lk1_ref,
                 lq2_ref, lk2_ref, g_ref, o_ref, m_sc, l_sc, acc_sc, lp_sc,
                 *, tq, tk, seq, n_heads, lam_init):
    b = pl.program_id(0)
    h = pl.program_id(1)
    bh = b * n_heads + h
    slope = slopes_ref[h]
    band = band_ref[bh]
    nk = seq // tk
    nch = tk // LANES
    q0 = pl.program_id(2) * tq
    kd = lax.div(q0, tk)
    lo = jnp.maximum(kd - band, 0)
    hi = jnp.minimum(kd + 1 + band, nk)

    q = q_ref[...]
    lane = lax.broadcasted_iota(jnp.int32, (tq, V_HEAD_DIM), 1)
    zero = jnp.zeros_like(q)
    qm = jnp.concatenate([jnp.where(lane < QK_HALF_DIM, q, zero),
                          jnp.where(lane >= QK_HALF_DIM, q, zero)], axis=0)
    d0 = (lax.broadcasted_iota(jnp.int32, (tq, tk), 0)
          - lax.broadcasted_iota(jnp.int32, (tq, tk), 1))

    def scores(k0):
        return lax.dot_general(qm, k_ref[pl.ds(k0, tk), :], (((1,), (1,)), ((), ())),
                               preferred_element_type=F32)

    def abs_bias(k0):
        bias = slope * jnp.abs(d0 + (q0 - k0)).astype(F32)
        return jnp.concatenate([bias, bias], axis=0)

    def finalize(l):
        lam = (jnp.exp(jnp.sum(lq1_ref[...] * lk1_ref[...], axis=1, keepdims=True))
               - jnp.exp(jnp.sum(lq2_ref[...] * lk2_ref[...], axis=1, keepdims=True))
               + lam_init)
        o = acc_sc[...] / l
        o = o[:tq] - lam * o[tq:]
        inv = lax.rsqrt(jnp.mean(o * o, axis=-1, keepdims=True) + NORM_EPS)
        o_ref[...] = ((o * inv * g_ref[...]) * (1.0 - lam_init)).astype(o_ref.dtype)

    @pl.when(fast_ref[bh] != 0)
    def _():
        qf = qm.astype(F32)
        nq = jnp.sqrt(jnp.sum(qf * qf, axis=1, keepdims=True))
        row = lax.broadcasted_iota(jnp.int32, (2 * tq, 1), 0)
        km = jnp.where(row < tq, kmax_ref[2 * bh], kmax_ref[2 * bh + 1])
        m_rep = jnp.broadcast_to(nq * km, (2 * tq, LANES))
        il = lax.broadcasted_iota(jnp.int32, (2 * tq, LANES), 0)
        il = slope * jnp.where(il >= tq, il - tq, il).astype(F32)
        jl = lax.broadcasted_iota(jnp.int32, (1, tk), 1).astype(F32)
        col_l = slope * (jl - (tk - 1))
        col_r = -slope * jl
        acc_sc[...] = jnp.zeros(acc_sc.shape, F32)
        lp_sc[...] = jnp.zeros(lp_sc.shape, F32)

        def tiled(x):
            return jnp.concatenate([x] * nch, axis=1)

        def weights(e, k0):
            p = jnp.exp2(e)
            lp = p[:, 0:LANES]
            for c in range(1, nch):
                lp = lp + p[:, c * LANES:(c + 1) * LANES]
            return lp, jnp.dot(p.astype(BF16), v_ref[pl.ds(k0, tk), :], preferred_element_type=F32)

        def accumulate(*parts):
            lp_sc[...] += functools.reduce(jnp.add, [lp for lp, _ in parts])
            acc_sc[...] += functools.reduce(jnp.add, [pv for _, pv in parts])

        def left(kj):
            k0 = pl.multiple_of(kj * tk, tk)
            mrow = (m_rep + il) - slope * (k0 + (tk - 1) - q0).astype(F32)
            return weights((scores(k0) + col_l) - tiled(mrow), k0)

        def right(kj):
            k0 = pl.multiple_of(kj * tk, tk)
            mrow = (m_rep - il) - slope * (q0 - k0).astype(F32)
            return weights((scores(k0) + col_r) - tiled(mrow), k0)

        def sweep(start, stop, one):
            count = stop - start

            def pair(t, c):
                kj = start + 2 * t
                accumulate(one(kj), one(kj + 1))
                return c

            lax.fori_loop(0, lax.shift_right_logical(count, 1), pair, 0)

            @pl.when(lax.bitwise_and(count, 1) == 1)
            def _():
                accumulate(one(stop - 1))

        k0d = pl.multiple_of(kd * tk, tk)
        accumulate(weights((scores(k0d) - abs_bias(k0d)) - tiled(m_rep), k0d))
        sweep(lo, kd, left)
        sweep(kd + 1, hi, right)
        finalize(jnp.sum(lp_sc[...], axis=1, keepdims=True))

    @pl.when(fast_ref[bh] == 0)
    def _():
        m_sc[...] = jnp.full(m_sc.shape, -jnp.inf, F32)
        l_sc[...] = jnp.zeros(l_sc.shape, F32)
        acc_sc[...] = jnp.zeros(acc_sc.shape, F32)

        def body(kj, c):
            k0 = pl.multiple_of(kj * tk, tk)
            s = scores(k0) - abs_bias(k0)
            m_prev = m_sc[...]
            m_new = jnp.maximum(m_prev, jnp.max(s, axis=1, keepdims=True))
            alpha = jnp.exp2(m_prev - m_new)
            p = jnp.exp2(s - m_new)
            l_sc[...] = alpha * l_sc[...] + jnp.sum(p, axis=1, keepdims=True)
            acc_sc[...] = alpha * acc_sc[...] + jnp.dot(p.astype(BF16), v_ref[pl.ds(k0, tk), :],
                                                         preferred_element_type=F32)
            m_sc[...] = m_new
            return c

        lax.fori_loop(lo, hi, body, 0)
        finalize(l_sc[...])


def _diff_attention(qkv, batch, seq, n_heads, slopes, lq1, lk1, lq2, lk2, subln_g, lam_init):
    n = batch * seq
    attn_width = n_heads * V_HEAD_DIM
    tq = _tile(seq, ATTN_TQ)
    tk = _tile(seq, max(tq, LANES, min(ATTN_TK, seq // 4)))
    assert tk % tq == 0 and tk % LANES == 0
    nq = seq // tq

    sq = _qk_bounds(qkv, batch, seq, attn_width)
    norms = jnp.sqrt(sq[:, :, :2 * n_heads]).reshape(batch, 2, n_heads, 2) * NORM_SLACK
    qmax, kmax = norms[:, 0], norms[:, 1]
    bmax = jnp.max(qmax * kmax, axis=-1)
    slopes2 = slopes * LOG2E
    band = jnp.floor((2.0 * bmax + SKIP_LOG2) / (slopes2[None, :] * tk)) + 1.0
    band = jnp.minimum(band, seq // tk).astype(jnp.int32).reshape(-1)
    fast = (2.0 * bmax <= FAST_LOG2).astype(jnp.int32).reshape(-1)

    kernel = functools.partial(_attn_kernel, tq=tq, tk=tk, seq=seq, n_heads=n_heads,
                               lam_init=lam_init)
    vec = lambda v: v.reshape(1, -1).astype(F32)
    small = lambda w: pl.BlockSpec((1, w), lambda b, h, i, *_: (0, 0))
    grid_spec = pltpu.PrefetchScalarGridSpec(
        num_scalar_prefetch=4,
        grid=(batch, n_heads, nq),
        in_specs=[pl.BlockSpec((tq, V_HEAD_DIM), lambda b, h, i, *_: (b * nq + i, h)),
                  pl.BlockSpec((seq, V_HEAD_DIM), lambda b, h, i, *_: (b, n_heads + h)),
                  pl.BlockSpec((seq, V_HEAD_DIM), lambda b, h, i, *_: (b, 2 * n_heads + h)),
                  small(QK_HALF_DIM), small(QK_HALF_DIM), small(QK_HALF_DIM), small(QK_HALF_DIM),
                  small(V_HEAD_DIM)],
        out_specs=pl.BlockSpec((tq, V_HEAD_DIM), lambda b, h, i, *_: (b * nq + i, h)),
        scratch_shapes=[pltpu.VMEM((2 * tq, 1), F32), pltpu.VMEM((2 * tq, 1), F32),
                        pltpu.VMEM((2 * tq, V_HEAD_DIM), F32), pltpu.VMEM((2 * tq, LANES), F32)],
    )
    return pl.pallas_call(
        kernel,
        grid_spec=grid_spec,
        out_shape=jax.ShapeDtypeStruct((n, attn_width), BF16),
        compiler_params=_params("arbitrary", "arbitrary", "arbitrary"),
        name="diff_attn",
    )(slopes2, band, fast, kmax.reshape(-1), qkv, qkv, qkv,
      vec(lq1), vec(lk1), vec(lq2), vec(lk2), vec(subln_g))


def _gates_kernel(cur_ref, prev_ref, next_ref, cw_ref, cb_ref, wg_ref, bg_ref, lam_ref,
                  af_ref, uf_ref, ab_ref, ub_ref, ext_sc, *, t, n_blocks, blk):
    j = pl.program_id(1)
    nt = pl.num_programs(1)
    halo = SUBLANES
    ext_sc[0:halo, :] = jnp.where(j > 0, prev_ref[...], 0.0)
    ext_sc[halo:halo + t, :] = cur_ref[...]
    ext_sc[halo + t:2 * halo + t, :] = jnp.where(j < nt - 1, next_ref[...], 0.0)
    left = CONV_WIDTH // 2
    y = cb_ref[...]
    for c in range(CONV_WIDTH):
        y = y + ext_sc[halo - left + c:halo - left + c + t, :] * cw_ref[c:c + 1, :]
    nl = -lam_ref[...]
    sp = jnp.maximum(nl, 0.0) + jnp.log1p(jnp.exp(-jnp.abs(nl)))
    outs = ((af_ref, uf_ref), (ab_ref, ub_ref))
    for g in range(n_blocks):
        cs = slice(g * blk, (g + 1) * blk)
        xc = y[:, cs]
        pre = jnp.dot(xc.astype(BF16), wg_ref[g], preferred_element_type=F32) + bg_ref[g:g + 1, :]
        for d in range(2):
            r = jax.nn.sigmoid(pre[:, d * blk:(d + 1) * blk])
            i = jax.nn.sigmoid(pre[:, (2 + d) * blk:(3 + d) * blk])
            log_a = (-RG_C * r) * sp[d:d + 1, cs]
            a = jnp.exp(log_a)
            u = jnp.sqrt(1.0 - jnp.exp(2.0 * log_a)) * (i * xc)
            outs[d][0][:, cs] = a
            outs[d][1][:, cs] = u


def _rg_gates(xg, batch, seq, conv_w, conv_b, wg, bg, lam):
    n = batch * seq
    c = xg.shape[1] // 2
    n_blocks, blk = wg.shape[0], wg.shape[1]
    t = _tile(seq, 256)
    nt = seq // t
    hb = t // SUBLANES
    last = n // SUBLANES - 1
    kernel = functools.partial(_gates_kernel, t=t, n_blocks=n_blocks, blk=blk)
    full = lambda shape: pl.BlockSpec(shape, lambda b, j: (0,) * len(shape))
    out_spec = pl.BlockSpec((t, c), lambda b, j: (b * nt + j, 0))
    return pl.pallas_call(
        kernel,
        grid=(batch, nt),
        in_specs=[pl.BlockSpec((t, c), lambda b, j: (b * nt + j, 0)),
                  pl.BlockSpec((SUBLANES, c), lambda b, j: (jnp.maximum((b * nt + j) * hb - 1, 0), 0)),
                  pl.BlockSpec((SUBLANES, c), lambda b, j: (jnp.minimum((b * nt + j + 1) * hb, last), 0)),
                  full((CONV_WIDTH, c)), full((1, c)),
                  full(wg.shape), full(bg.shape), full((2, c))],
        out_specs=[out_spec] * 4,
        out_shape=[jax.ShapeDtypeStruct((n, c), F32)] * 4,
        scratch_shapes=[pltpu.VMEM((t + 2 * SUBLANES, c), F32)],
        compiler_params=_params("arbitrary", "arbitrary"),
        name="rg_gates",
    )(xg, xg, xg, conv_w, conv_b.reshape(1, c), wg, bg, lam)


def _scan_fwd_kernel(a_ref, u_ref, h_ref, carry_sc, *, t):
    @pl.when(pl.program_id(1) == 0)
    def _():
        carry_sc[...] = jnp.zeros(carry_sc.shape, F32)

    def body(i, h):
        h = a_ref[pl.ds(i, 1), :] * h + u_ref[pl.ds(i, 1), :]
        h_ref[pl.ds(i, 1), :] = h
        return h

    carry_sc[...] = lax.fori_loop(0, t, body, carry_sc[...], unroll=8)


def _scan_fwd(a, u, batch, seq):
    n, c = a.shape
    t = _tile(seq, 256)
    nt = seq // t
    spec = pl.BlockSpec((t, c), lambda b, j: (b * nt + j, 0))
    return pl.pallas_call(
        functools.partial(_scan_fwd_kernel, t=t),
        grid=(batch, nt),
        in_specs=[spec, spec],
        out_specs=spec,
        out_shape=jax.ShapeDtypeStruct((n, c), F32),
        scratch_shapes=[pltpu.VMEM((1, c), F32)],
        compiler_params=_params("arbitrary", "arbitrary"),
        name="scan_fwd",
    )(a, u)


def _scan_bwd_kernel(a_ref, u_ref, hf_ref, gr_ref, g_ref, o_ref, carry_sc, hb_sc, *, t):
    @pl.when(pl.program_id(1) == 0)
    def _():
        carry_sc[...] = jnp.zeros(carry_sc.shape, F32)

    def body(i, h):
        r = t - 1 - i
        h = a_ref[pl.ds(r, 1), :] * h + u_ref[pl.ds(r, 1), :]
        hb_sc[pl.ds(r, 1), :] = h
        return h

    carry_sc[...] = lax.fori_loop(0, t, body, carry_sc[...], unroll=8)
    rec = (hf_ref[...] + hb_sc[...]) * jax.nn.gelu(gr_ref[...], approximate=True)
    inv = lax.rsqrt(jnp.mean(rec * rec, axis=-1, keepdims=True) + NORM_EPS)
    o_ref[...] = (rec * inv * g_ref[...]).astype(o_ref.dtype)


def _scan_bwd_finish(a, u, hf, xg, norm_g, batch, seq):
    n, c = a.shape
    t = _tile(seq, 256)
    nt = seq // t
    rev = lambda b, j: (b * nt + (nt - 1 - j), 0)
    spec = pl.BlockSpec((t, c), rev)
    return pl.pallas_call(
        functools.partial(_scan_bwd_kernel, t=t),
        grid=(batch, nt),
        in_specs=[spec, spec, spec,
                  pl.BlockSpec((t, c), lambda b, j: (b * nt + (nt - 1 - j), 1)),
                  pl.BlockSpec((1, c), lambda b, j: (0, 0))],
        out_specs=spec,
        out_shape=jax.ShapeDtypeStruct((n, c), BF16),
        scratch_shapes=[pltpu.VMEM((1, c), F32), pltpu.VMEM((t, c), F32)],
        compiler_params=_params("arbitrary", "arbitrary"),
        name="scan_bwd",
    )(a, u, hf, xg, norm_g.reshape(1, c))


def _router_kernel(h_ref, g_ref, rw_ref, hn_ref, aff_ref, *, n_experts):
    x = h_ref[...]
    inv = lax.rsqrt(jnp.mean(x * x, axis=-1, keepdims=True) + NORM_EPS)
    hn = x * inv * g_ref[...]
    hn_ref[...] = hn
    logits = jnp.dot(hn.astype(BF16), rw_ref[...], preferred_element_type=F32)
    lane = lax.broadcasted_iota(jnp.int32, logits.shape, 1)
    logits = jnp.where(lane < n_experts, logits, -jnp.inf)
    e = jnp.exp(logits - jnp.max(logits, axis=-1, keepdims=True))
    aff_ref[...] = e / jnp.sum(e, axis=-1, keepdims=True)


def _router(h, g, rw_pad, n_experts):
    n, d = h.shape
    tm = _tile(n, 256)
    return pl.pallas_call(
        functools.partial(_router_kernel, n_experts=n_experts),
        grid=(n // tm,),
        in_specs=[pl.BlockSpec((tm, d), lambda i: (i, 0)),
                  pl.BlockSpec((1, d), lambda i: (0, 0)),
                  pl.BlockSpec((d, LANES), lambda i: (0, 0))],
        out_specs=[pl.BlockSpec((tm, d), lambda i: (i, 0)),
                   pl.BlockSpec((tm, LANES), lambda i: (i, 0))],
        out_shape=[jax.ShapeDtypeStruct((n, d), F32), jax.ShapeDtypeStruct((n, LANES), F32)],
        compiler_params=_params("arbitrary"),
        name="router",
    )(h, g.reshape(1, d), rw_pad)


def _route_kernel(aff_ref, dest_ref, *, cap, n_tok, chunk):
    n_chunks = n_tok // chunk

    def bits_of(c):
        return lax.bitcast_convert_type(aff_ref[pl.ds(pl.multiple_of(c * chunk, chunk), chunk), :],
                                        jnp.int32)

    def count(pred):
        def body(c, acc):
            return acc + jnp.sum(jnp.where(pred(bits_of(c)), 1.0, 0.0), axis=0, keepdims=True)
        return lax.fori_loop(0, n_chunks, body, jnp.zeros((1, LANES), F32))

    def search(i, thr):
        cand = thr | lax.shift_left(jnp.int32(1), 30 - i)
        return jnp.where(count(lambda v: v >= cand) >= cap, cand, thr)

    thr = lax.fori_loop(0, 31, search, jnp.zeros((1, LANES), jnp.int32))
    need = cap - count(lambda v: v > thr)

    ltri = jnp.where(lax.broadcasted_iota(jnp.int32, (chunk, chunk), 0)
                     > lax.broadcasted_iota(jnp.int32, (chunk, chunk), 1), 1.0, 0.0).astype(BF16)

    def place(c, carry):
        ties_before, taken_before = carry
        v = bits_of(c)
        tie = v == thr
        tie_f = jnp.where(tie, 1.0, 0.0)
        tie_rank = jnp.dot(ltri, tie_f.astype(BF16), preferred_element_type=F32) + ties_before
        take = jnp.logical_or(v > thr, jnp.logical_and(tie, tie_rank < need))
        take_f = jnp.where(take, 1.0, 0.0)
        pos = jnp.dot(ltri, take_f.astype(BF16), preferred_element_type=F32) + taken_before
        dest_ref[pl.ds(pl.multiple_of(c * chunk, chunk), chunk), :] = jnp.where(
            take, pos.astype(jnp.int32), cap)
        return (ties_before + jnp.sum(tie_f, axis=0, keepdims=True),
                taken_before + jnp.sum(take_f, axis=0, keepdims=True))

    zero = jnp.zeros((1, LANES), F32)
    lax.fori_loop(0, n_chunks, place, (zero, zero))


def _route(aff_pad, cap):
    n_tok = aff_pad.shape[0]
    chunk = _tile(n_tok, LANES)
    return pl.pallas_call(
        functools.partial(_route_kernel, cap=cap, n_tok=n_tok, chunk=chunk),
        out_shape=jax.ShapeDtypeStruct((n_tok, LANES), jnp.int32),
        compiler_params=pltpu.CompilerParams(vmem_limit_bytes=VMEM_LIMIT_BYTES),
        name="route",
    )(aff_pad)


def _compact_kernel(dest_ref, bits_ref, idx_ref, gate_ref, *, n_tok, cap):
    def clear(i, c):
        idx_ref[i] = 0
        gate_ref[i] = 0
        return c

    lax.fori_loop(cap, idx_ref.shape[0], clear, 0)

    def body(n, c):
        idx_ref[dest_ref[n]] = n
        return c

    lax.fori_loop(0, n_tok, body, 0, unroll=16)

    def gate(s, c):
        gate_ref[s] = bits_ref[idx_ref[s]]
        return c

    lax.fori_loop(0, cap, gate, 0, unroll=16)


def _compact(dest_flat, bits_flat, n_experts, n_tok, cap):
    width = cap + SMEM_TILE_1D
    in_spec = pl.BlockSpec((n_tok,), lambda e: (e,), memory_space=pltpu.SMEM)
    out_spec = pl.BlockSpec((width,), lambda e: (e,), memory_space=pltpu.SMEM)
    out = jax.ShapeDtypeStruct((n_experts * width,), jnp.int32)
    idx, gate = pl.pallas_call(
        functools.partial(_compact_kernel, n_tok=n_tok, cap=cap),
        grid=(n_experts,),
        in_specs=[in_spec, in_spec],
        out_specs=[out_spec, out_spec],
        out_shape=[out, out],
        compiler_params=_params("arbitrary"),
        name="compact",
    )(dest_flat, bits_flat)
    return idx.reshape(n_experts, width), gate.reshape(n_experts, width)


def _row_copy(src, dst, src_row, dst_row, sem):
    return pltpu.make_async_copy(src.at[pl.ds(src_row, 1), :], dst.at[pl.ds(dst_row, 1), :], sem)


def _tile_wait(src, dst, sem, rows):
    pltpu.make_async_copy(src.at[pl.ds(0, rows), :], dst.at[pl.ds(0, rows), :], sem).wait()


def _gather_kernel(idx_ref, src_ref, o_ref, buf, sem, *, tg):
    i = pl.program_id(0)
    n = pl.num_programs(0)

    def issue(tile, slot):
        def start(r, c):
            _row_copy(src_ref, buf.at[slot], idx_ref[tile * tg + r], r, sem.at[slot]).start()
            return c
        lax.fori_loop(0, tg, start, 0, unroll=8)

    @pl.when(i == 0)
    def _():
        issue(0, 0)

    @pl.when(i + 1 < n)
    def _():
        issue(i + 1, (i + 1) % 2)

    slot = i % 2
    _tile_wait(src_ref, buf.at[slot], sem.at[slot], tg)
    o_ref[...] = buf[slot].astype(o_ref.dtype)


def _gather_rows(src, idx, out_dtype):
    n_slots = idx.shape[0]
    d = src.shape[1]
    tg = _tile(n_slots, GATHER_ROWS)
    grid_spec = pltpu.PrefetchScalarGridSpec(
        num_scalar_prefetch=1,
        grid=(n_slots // tg,),
        in_specs=[pl.BlockSpec(memory_space=pl.ANY)],
        out_specs=pl.BlockSpec((tg, d), lambda i, s: (i, 0)),
        scratch_shapes=[pltpu.VMEM((2, tg, d), src.dtype), pltpu.SemaphoreType.DMA((2,))],
    )
    return pl.pallas_call(
        functools.partial(_gather_kernel, tg=tg),
        grid_spec=grid_spec,
        out_shape=jax.ShapeDtypeStruct((n_slots, d), out_dtype),
        compiler_params=_params("arbitrary"),
        name="gather_rows",
    )(idx, src)


SCATTER_SLOTS = 3


def _scatter_add_kernel(idx_ref, ye_ref, h_in_ref, h_ref, buf, rsem, wsem, *, tg, tiles_per_expert):
    del h_in_ref
    i = pl.program_id(0)
    first = i % tiles_per_expert == 0
    last = i % tiles_per_expert == tiles_per_expert - 1

    def read(tile):
        slot = tile % SCATTER_SLOTS

        def start(r, c):
            _row_copy(h_ref, buf.at[slot], idx_ref[tile * tg + r], r, rsem.at[slot]).start()
            return c
        lax.fori_loop(0, tg, start, 0, unroll=8)

    def write(tile):
        slot = tile % SCATTER_SLOTS

        def start(r, c):
            _row_copy(buf.at[slot], h_ref, r, idx_ref[tile * tg + r], wsem.at[slot]).start()
            return c
        lax.fori_loop(0, tg, start, 0, unroll=8)

    def wait_write(tile):
        slot = tile % SCATTER_SLOTS
        _tile_wait(buf.at[slot], h_ref, wsem.at[slot], tg)

    @pl.when(first)
    def _():
        read(i)

    @pl.when(jnp.logical_not(last))
    def _():
        read(i + 1)

    slot = i % SCATTER_SLOTS
    _tile_wait(h_ref, buf.at[slot], rsem.at[slot], tg)
    buf[slot] = buf[slot] + ye_ref[...]
    write(i)

    @pl.when(jnp.logical_not(first))
    def _():
        wait_write(i - 1)

    @pl.when(last)
    def _():
        wait_write(i)


def _scatter_add(h, ye, idx, cap):
    n_slots, d = ye.shape
    tg = _tile(cap, SCATTER_ROWS)
    grid_spec = pltpu.PrefetchScalarGridSpec(
        num_scalar_prefetch=1,
        grid=(n_slots // tg,),
        in_specs=[pl.BlockSpec((tg, d), lambda i, s: (i, 0)),
                  pl.BlockSpec(memory_space=pl.ANY)],
        out_specs=pl.BlockSpec(memory_space=pl.ANY),
        scratch_shapes=[pltpu.VMEM((SCATTER_SLOTS, tg, d), F32),
                        pltpu.SemaphoreType.DMA((SCATTER_SLOTS,)),
                        pltpu.SemaphoreType.DMA((SCATTER_SLOTS,))],
    )
    return pl.pallas_call(
        functools.partial(_scatter_add_kernel, tg=tg, tiles_per_expert=cap // tg),
        grid_spec=grid_spec,
        out_shape=jax.ShapeDtypeStruct(h.shape, h.dtype),
        input_output_aliases={2: 0},
        compiler_params=_params("arbitrary"),
        name="scatter_add",
    )(idx, ye, h)


def _ffn_up_kernel(x_ref, w1_ref, w3_ref, o_ref):
    x = x_ref[...]
    a = jnp.dot(x, w1_ref[...].astype(BF16), preferred_element_type=F32)
    b = jnp.dot(x, w3_ref[...].astype(BF16), preferred_element_type=F32)
    o_ref[...] = (jax.nn.silu(a) * b).astype(o_ref.dtype)


def _ffn_up(xe, w1, w3, cap, tn=256):
    n_experts, d, f = w1.shape
    tn = _tile(f, tn)
    wspec = pl.BlockSpec((None, d, tn), lambda e, j: (e, 0, j))
    return pl.pallas_call(
        _ffn_up_kernel,
        grid=(n_experts, f // tn),
        in_specs=[pl.BlockSpec((cap, d), lambda e, j: (e, 0), pipeline_mode=pl.Buffered(1)),
                  wspec, wspec],
        out_specs=pl.BlockSpec((cap, tn), lambda e, j: (e, j)),
        out_shape=jax.ShapeDtypeStruct((n_experts * cap, f), BF16),
        compiler_params=_params("arbitrary", "arbitrary"),
        name="ffn_up",
    )(xe, w1, w3)


def _ffn_down_kernel(h_ref, w2_ref, gate_ref, o_ref):
    acc = jnp.dot(h_ref[...], w2_ref[...].astype(BF16), preferred_element_type=F32)
    o_ref[...] = acc * gate_ref[...]


def _ffn_down(hm, w2, gates, cap, tn=256):
    n_experts, f, d = w2.shape
    tn = _tile(d, tn)
    return pl.pallas_call(
        _ffn_down_kernel,
        grid=(n_experts, d // tn),
        in_specs=[pl.BlockSpec((cap, f), lambda e, j: (e, 0), pipeline_mode=pl.Buffered(1)),
                  pl.BlockSpec((None, f, tn), lambda e, j: (e, 0, j)),
                  pl.BlockSpec((cap, 1), lambda e, j: (e, 0), pipeline_mode=pl.Buffered(1))],
        out_specs=pl.BlockSpec((cap, tn), lambda e, j: (e, j)),
        out_shape=jax.ShapeDtypeStruct((n_experts * cap, d), F32),
        compiler_params=_params("arbitrary", "arbitrary"),
        name="ffn_down",
    )(hm, w2, gates)


def _prepare_weights(rg_wa, rg_ba, rg_wx, rg_bx, router_w):
    n_blocks, blk = rg_wa.shape[1], rg_wa.shape[2]
    wg = jnp.concatenate([rg_wa[0], rg_wa[1], rg_wx[0], rg_wx[1]], axis=-1).astype(BF16)
    bg = jnp.concatenate([rg_ba[0].reshape(n_blocks, blk), rg_ba[1].reshape(n_blocks, blk),
                          rg_bx[0].reshape(n_blocks, blk), rg_bx[1].reshape(n_blocks, blk)], axis=-1)
    n_experts = router_w.shape[1]
    rw_pad = jnp.pad(router_w, ((0, 0), (0, LANES - n_experts))).astype(BF16)
    return dict(wg=wg, bg=bg, rw_pad=rw_pad)


def _layer(x, layer, pw, norm_mix_g, w_in, lambda_q1, lambda_k1, lambda_q2, lambda_k2,
           attn_subln_g, conv_w, conv_b, rg_lambda, rec_norm_g, w_out, norm_ffn_g,
           exp_w1, exp_w3, exp_w2, n_experts, attn_width):
    batch, seq, d = x.shape
    n = batch * seq
    n_heads = attn_width // V_HEAD_DIM
    rec_width = conv_w.shape[-1]
    x2 = x.reshape(n, d)

    xn = _rmsnorm(x2, norm_mix_g, BF16)
    q_scale = QK_HALF_DIM ** -0.5 * LOG2E
    qkv_scale = jnp.concatenate([jnp.full((1, attn_width), q_scale, F32),
                                 jnp.ones((1, 2 * attn_width), F32)], axis=1)
    qkv = _matmul(xn, w_in, 0, 3 * attn_width, qkv_scale, BF16)
    xg = _matmul(xn, w_in, 3 * attn_width, 2 * rec_width, jnp.ones((1, 2 * rec_width), F32), F32)
    lam_init = 0.8 - 0.6 * math.exp(-0.3 * layer)
    slopes = 2.0 ** (-8.0 * jnp.arange(1, n_heads + 1, dtype=F32) / n_heads)
    attn = _diff_attention(qkv, batch, seq, n_heads, slopes, lambda_q1, lambda_k1, lambda_q2,
                           lambda_k2, attn_subln_g, lam_init)
    a_f, u_f, a_b, u_b = _rg_gates(xg, batch, seq, conv_w, conv_b, pw["wg"], pw["bg"], rg_lambda)
    h_f = _scan_fwd(a_f, u_f, batch, seq)
    rec = _scan_bwd_finish(a_b, u_b, h_f, xg, rec_norm_g, batch, seq)
    h = _outproj(attn, rec, w_out, x2)

    cap = (EC_CAPACITY_FACTOR * n) // n_experts
    hn, aff = _router(h, norm_ffn_g, pw["rw_pad"], n_experts)
    dest = _route(aff, cap)
    bits_t = lax.bitcast_convert_type(aff[:, :n_experts].T, jnp.int32)
    idx, gate_bits = _compact(dest[:, :n_experts].T.reshape(-1), bits_t.reshape(-1),
                              n_experts, n, cap)
    idx = idx[:, :cap].reshape(-1)
    gates = lax.bitcast_convert_type(gate_bits[:, :cap], F32).reshape(-1, 1)
    xe = _gather_rows(hn, idx, BF16)
    hm = _ffn_up(xe, exp_w1, exp_w3, cap)
    ye = _ffn_down(hm, exp_w2, gates, cap)
    return _scatter_add(h, ye, idx, cap)


def _trunk(x, pw, norm_mix_g, w_in, lambda_q1, lambda_k1, lambda_q2, lambda_k2, attn_subln_g,
           conv_w, conv_b, rg_lambda, rec_norm_g, w_out, norm_ffn_g, exp_w1, exp_w3, exp_w2,
           final_norm_g, n_experts, attn_width):
    batch, seq, d = x.shape
    h = x
    for l in range(norm_mix_g.shape[0]):
        h = _layer(h.reshape(batch, seq, d), l, pw[l], norm_mix_g[l], w_in[l], lambda_q1[l],
                   lambda_k1[l], lambda_q2[l], lambda_k2[l], attn_subln_g[l], conv_w[l], conv_b[l],
                   rg_lambda[l], rec_norm_g[l], w_out[l], norm_ffn_g[l], exp_w1[l], exp_w3[l],
                   exp_w2[l], n_experts, attn_width)
    return _rmsnorm(h.reshape(batch * seq, d), final_norm_g, F32).reshape(batch, seq, d)


def kernel(x_prompt, x_sample, norm_mix_g, w_in, lambda_q1, lambda_k1, lambda_q2, lambda_k2, attn_subln_g, conv_w, conv_b, rg_wa, rg_ba, rg_wx, rg_bx, rg_lambda, rec_norm_g, w_out, norm_ffn_g, router_w, exp_w1, exp_w3, exp_w2, final_norm_g):
    depth = w_in.shape[0]
    rec_width = conv_w.shape[-1]
    attn_width = (w_in.shape[-1] - 2 * rec_width) // 3
    n_experts = router_w.shape[-1]
    pw = [_prepare_weights(rg_wa[l], rg_ba[l], rg_wx[l], rg_bx[l], router_w[l]) for l in range(depth)]
    run = lambda x: _trunk(x, pw, norm_mix_g, w_in, lambda_q1, lambda_k1, lambda_q2, lambda_k2,
                           attn_subln_g, conv_w, conv_b, rg_lambda, rec_norm_g, w_out, norm_ffn_g,
                           exp_w1, exp_w3, exp_w2, final_norm_g, n_experts, attn_width)
    return (run(x_prompt), run(x_sample))
```

```python
import functools
import math

import jax
import jax.numpy as jnp
from jax import lax
from jax.experimental import pallas as pl
from jax.experimental.pallas import tpu as pltpu

V_HEAD_DIM = 128
QK_HALF_DIM = V_HEAD_DIM // 2
CONV_WIDTH = 4
RG_C = 8.0
EC_CAPACITY_FACTOR = 2
NORM_EPS = 1e-6
LANES = 128
SUBLANES = 8
SMEM_TILE_1D = 1024
VMEM_LIMIT_BYTES = 52 * 1024 * 1024

GATHER_ROWS = 512
SCATTER_ROWS = 256
ATTN_TQ = 512
ATTN_TK = 1024
LOG2E = math.log2(math.e)
SKIP_LOG2 = 40.0
FAST_LOG2 = 110.0
NORM_SLACK = 1.01

F32 = jnp.float32
BF16 = jnp.bfloat16


def _params(*sem):
    return pltpu.CompilerParams(dimension_semantics=sem, vmem_limit_bytes=VMEM_LIMIT_BYTES)


def _tile(n, want):
    t = min(n, want)
    while n % t:
        t //= 2
    return t


def _rms_kernel(x_ref, g_ref, o_ref):
    x = x_ref[...]
    inv = lax.rsqrt(jnp.mean(x * x, axis=-1, keepdims=True) + NORM_EPS)
    o_ref[...] = (x * inv * g_ref[...]).astype(o_ref.dtype)


def _rmsnorm(x, g, out_dtype):
    n, d = x.shape
    tm = _tile(n, 256)
    return pl.pallas_call(
        _rms_kernel,
        grid=(n // tm,),
        in_specs=[pl.BlockSpec((tm, d), lambda i: (i, 0)),
                  pl.BlockSpec((1, d), lambda i: (0, 0))],
        out_specs=pl.BlockSpec((tm, d), lambda i: (i, 0)),
        out_shape=jax.ShapeDtypeStruct((n, d), out_dtype),
        compiler_params=_params("arbitrary"),
        name="rmsnorm",
    )(x, g.reshape(1, d))


def _mm_kernel(x_ref, w_ref, s_ref, o_ref, wb_sc):
    @pl.when(pl.program_id(1) == 0)
    def _():
        wb_sc[...] = w_ref[...].astype(BF16)

    acc = jnp.dot(x_ref[...], wb_sc[...], preferred_element_type=F32)
    o_ref[...] = (acc * s_ref[...]).astype(o_ref.dtype)


def _matmul(x, w, col0, n, col_scale, out_dtype, tm=1024, tn=512):
    m, k = x.shape
    tm, tn = _tile(m, tm), _tile(math.gcd(n, col0), tn)
    jb = col0 // tn
    return pl.pallas_call(
        _mm_kernel,
        grid=(n // tn, m // tm),
        in_specs=[pl.BlockSpec((tm, k), lambda j, i: (i, 0)),
                  pl.BlockSpec((k, tn), lambda j, i: (0, jb + j)),
                  pl.BlockSpec((1, tn), lambda j, i: (0, j))],
        out_specs=pl.BlockSpec((tm, tn), lambda j, i: (i, j)),
        out_shape=jax.ShapeDtypeStruct((m, n), out_dtype),
        scratch_shapes=[pltpu.VMEM((k, tn), BF16)],
        compiler_params=_params("arbitrary", "arbitrary"),
        name="in_proj",
    )(x, w, col_scale)


def _outproj_kernel(a_ref, r_ref, wt_ref, wb_ref, x_ref, o_ref, wt_sc, wb_sc):
    @pl.when(pl.program_id(1) == 0)
    def _():
        wt_sc[...] = wt_ref[...].astype(BF16)
        wb_sc[...] = wb_ref[...].astype(BF16)

    acc = jnp.dot(a_ref[...], wt_sc[...], preferred_element_type=F32)
    acc = acc + jnp.dot(r_ref[...], wb_sc[...], preferred_element_type=F32)
    o_ref[...] = x_ref[...] + acc


def _outproj(attn, rec, w, x, tm=1024, tn=512):
    m, ka = attn.shape
    kr = rec.shape[1]
    assert ka == kr
    n = w.shape[1]
    tm, tn = _tile(m, tm), _tile(n, tn)
    return pl.pallas_call(
        _outproj_kernel,
        grid=(n // tn, m // tm),
        in_specs=[pl.BlockSpec((tm, ka), lambda j, i: (i, 0)),
                  pl.BlockSpec((tm, kr), lambda j, i: (i, 0)),
                  pl.BlockSpec((ka, tn), lambda j, i: (0, j)),
                  pl.BlockSpec((kr, tn), lambda j, i: (1, j)),
                  pl.BlockSpec((tm, tn), lambda j, i: (i, j))],
        out_specs=pl.BlockSpec((tm, tn), lambda j, i: (i, j)),
        out_shape=jax.ShapeDtypeStruct((m, n), F32),
        scratch_shapes=[pltpu.VMEM((ka, tn), BF16), pltpu.VMEM((kr, tn), BF16)],
        compiler_params=_params("arbitrary", "arbitrary"),
        name="out_proj",
    )(attn, rec, w, w, x)


def _qk_bound_kernel(q_ref, k_ref, g_ref, o_ref):
    @pl.when(pl.program_id(1) == 0)
    def _():
        o_ref[...] = jnp.zeros(o_ref.shape, F32)

    def group_max(x_ref):
        x = x_ref[...].astype(F32)
        sq = x * x
        hi = sq.astype(BF16)
        lo = (sq - hi.astype(F32)).astype(BF16)
        s = (jnp.dot(hi, g_ref[...], preferred_element_type=F32)
             + jnp.dot(lo, g_ref[...], preferred_element_type=F32))
        return jnp.max(s, axis=0, keepdims=True)

    o_ref[0:1, :] = jnp.maximum(o_ref[0:1, :], group_max(q_ref))
    o_ref[1:2, :] = jnp.maximum(o_ref[1:2, :], group_max(k_ref))


def _qk_bounds(qkv, batch, seq, attn_width):
    n_groups = attn_width // QK_HALF_DIM
    assert n_groups <= LANES
    tm = _tile(seq, 512)
    nt = seq // tm
    g = (jnp.arange(attn_width)[:, None] // QK_HALF_DIM == jnp.arange(LANES)[None, :]).astype(BF16)
    return pl.pallas_call(
        _qk_bound_kernel,
        grid=(batch, nt),
        in_specs=[pl.BlockSpec((tm, attn_width), lambda b, j: (b * nt + j, 0)),
                  pl.BlockSpec((tm, attn_width), lambda b, j: (b * nt + j, 1)),
                  pl.BlockSpec((attn_width, LANES), lambda b, j: (0, 0))],
        out_specs=pl.BlockSpec((None, 2, LANES), lambda b, j: (b, 0, 0)),
        out_shape=jax.ShapeDtypeStruct((batch, 2, LANES), F32),
        compiler_params=_params("arbitrary", "arbitrary"),
        name="qk_bounds",
    )(qkv, qkv, g)


def _attn_kernel(slopes_ref, band_ref, fast_ref, kmax_ref, q_ref, k_ref, v_ref, lq1_ref, lk1_ref,
                 lq2_ref, lk2_ref, g_ref, o_ref, m_sc, l_sc, acc_sc, lp_sc,
                 *, tq, tk, seq, n_heads, lam_init):
    b = pl.program_id(0)
    h = pl.program_id(1)
    bh = b * n_heads + h
    slope = slopes_ref[h]
    band = band_ref[bh]
    nk = seq // tk
    nch = tk // LANES
    q0 = pl.program_id(2) * tq
    kd = lax.div(q0, tk)
    lo = jnp.maximum(kd - band, 0)
    hi = jnp.minimum(kd + 1 + band, nk)

    q = q_ref[...]
    lane = lax.broadcasted_iota(jnp.int32, (tq, V_HEAD_DIM), 1)
    zero = jnp.zeros_like(q)
    qm = jnp.concatenate([jnp.where(lane < QK_HALF_DIM, q, zero),
                          jnp.where(lane >= QK_HALF_DIM, q, zero)], axis=0)
    d0 = (lax.broadcasted_iota(jnp.int32, (tq, tk), 0)
          - lax.broadcasted_iota(jnp.int32, (tq, tk), 1))

    def scores(k0):
        return lax.dot_general(qm, k_ref[pl.ds(k0, tk), :], (((1,), (1,)), ((), ())),
                               preferred_element_type=F32)

    def abs_bias(k0):
        bias = slope * jnp.abs(d0 + (q0 - k0)).astype(F32)
        return jnp.concatenate([bias, bias], axis=0)

    def finalize(l):
        lam = (jnp.exp(jnp.sum(lq1_ref[...] * lk1_ref[...], axis=1, keepdims=True))
               - jnp.exp(jnp.sum(lq2_ref[...] * lk2_ref[...], axis=1, keepdims=True))
               + lam_init)
        o = acc_sc[...] / l
        o = o[:tq] - lam * o[tq:]
        inv = lax.rsqrt(jnp.mean(o * o, axis=-1, keepdims=True) + NORM_EPS)
        o_ref[...] = ((o * inv * g_ref[...]) * (1.0 - lam_init)).astype(o_ref.dtype)

    @pl.when(fast_ref[bh] != 0)
    def _():
        qf = qm.astype(F32)
        nq = jnp.sqrt(jnp.sum(qf * qf, axis=1, keepdims=True))
        row = lax.broadcasted_iota(jnp.int32, (2 * tq, 1), 0)
        km = jnp.where(row < tq, kmax_ref[2 * bh], kmax_ref[2 * bh + 1])
        m_rep = jnp.broadcast_to(nq * km, (2 * tq, LANES))
        il = lax.broadcasted_iota(jnp.int32, (2 * tq, LANES), 0)
        il = slope * jnp.where(il >= tq, il - tq, il).astype(F32)
        jl = lax.broadcasted_iota(jnp.int32, (1, tk), 1).astype(F32)
        col_l = slope * (jl - (tk - 1))
        col_r = -slope * jl
        acc_sc[...] = jnp.zeros(acc_sc.shape, F32)
        lp_sc[...] = jnp.zeros(lp_sc.shape, F32)

        def tiled(x):
            return jnp.concatenate([x] * nch, axis=1)

        def weights(e, k0):
            p = jnp.exp2(e)
            lp = p[:, 0:LANES]
            for c in range(1, nch):
                lp = lp + p[:, c * LANES:(c + 1) * LANES]
            return lp, jnp.dot(p.astype(BF16), v_ref[pl.ds(k0, tk), :], preferred_element_type=F32)

        def accumulate(*parts):
            lp_sc[...] += functools.reduce(jnp.add, [lp for lp, _ in parts])
            acc_sc[...] += functools.reduce(jnp.add, [pv for _, pv in parts])

        def left(kj):
            k0 = pl.multiple_of(kj * tk, tk)
            mrow = (m_rep + il) - slope * (k0 + (tk - 1) - q0).astype(F32)
            return weights((scores(k0) + col_l) - tiled(mrow), k0)

        def right(kj):
            k0 = pl.multiple_of(kj * tk, tk)
            mrow = (m_rep - il) - slope * (q0 - k0).astype(F32)
            return weights((scores(k0) + col_r) - tiled(mrow), k0)

        def sweep(start, stop, one):
            count = stop - start

            def pair(t, c):
                kj = start + 2 * t
                accumulate(one(kj), one(kj + 1))
                return c

            lax.fori_loop(0, lax.shift_right_logical(count, 1), pair, 0)

            @pl.when(lax.bitwise_and(count, 1) == 1)
            def _():
                accumulate(one(stop - 1))

        k0d = pl.multiple_of(kd * tk, tk)
        accumulate(weights((scores(k0d) - abs_bias(k0d)) - tiled(m_rep), k0d))
        sweep(lo, kd, left)
        sweep(kd + 1, hi, right)
        finalize(jnp.sum(lp_sc[...], axis=1, keepdims=True))

    @pl.when(fast_ref[bh] == 0)
    def _():
        m_sc[...] = jnp.full(m_sc.shape, -jnp.inf, F32)
        l_sc[...] = jnp.zeros(l_sc.shape, F32)
        acc_sc[...] = jnp.zeros(acc_sc.shape, F32)

        def body(kj, c):
            k0 = pl.multiple_of(kj * tk, tk)
            s = scores(k0) - abs_bias(k0)
            m_prev = m_sc[...]
            m_new = jnp.maximum(m_prev, jnp.max(s, axis=1, keepdims=True))
            alpha = jnp.exp2(m_prev - m_new)
            p = jnp.exp2(s - m_new)
            l_sc[...] = alpha * l_sc[...] + jnp.sum(p, axis=1, keepdims=True)
            acc_sc[...] = alpha * acc_sc[...] + jnp.dot(p.astype(BF16), v_ref[pl.ds(k0, tk), :],
                                                         preferred_element_type=F32)
            m_sc[...] = m_new
            return c

        lax.fori_loop(lo, hi, body, 0)
        finalize(l_sc[...])


def _diff_attention(qkv, batch, seq, n_heads, slopes, lq1, lk1, lq2, lk2, subln_g, lam_init):
    n = batch * seq
    attn_width = n_heads * V_HEAD_DIM
    tq = _tile(seq, ATTN_TQ)
    tk = _tile(seq, max(tq, LANES, min(ATTN_TK, seq // 4)))
    assert tk % tq == 0 and tk % LANES == 0
    nq = seq // tq

    sq = _qk_bounds(qkv, batch, seq, attn_width)
    norms = jnp.sqrt(sq[:, :, :2 * n_heads]).reshape(batch, 2, n_heads, 2) * NORM_SLACK
    qmax, kmax = norms[:, 0], norms[:, 1]
    bmax = jnp.max(qmax * kmax, axis=-1)
    slopes2 = slopes * LOG2E
    band = jnp.floor((2.0 * bmax + SKIP_LOG2) / (slopes2[None, :] * tk)) + 1.0
    band = jnp.minimum(band, seq // tk).astype(jnp.int32).reshape(-1)
    fast = (2.0 * bmax <= FAST_LOG2).astype(jnp.int32).reshape(-1)

    kernel = functools.partial(_attn_kernel, tq=tq, tk=tk, seq=seq, n_heads=n_heads,
                               lam_init=lam_init)
    vec = lambda v: v.reshape(1, -1).astype(F32)
    small = lambda w: pl.BlockSpec((1, w), lambda b, h, i, *_: (0, 0))
    grid_spec = pltpu.PrefetchScalarGridSpec(
        num_scalar_prefetch=4,
        grid=(batch, n_heads, nq),
        in_specs=[pl.BlockSpec((tq, V_HEAD_DIM), lambda b, h, i, *_: (b * nq + i, h)),
                  pl.BlockSpec((seq, V_HEAD_DIM), lambda b, h, i, *_: (b, n_heads + h)),
                  pl.BlockSpec((seq, V_HEAD_DIM), lambda b, h, i, *_: (b, 2 * n_heads + h)),
                  small(QK_HALF_DIM), small(QK_HALF_DIM), small(QK_HALF_DIM), small(QK_HALF_DIM),
                  small(V_HEAD_DIM)],
        out_specs=pl.BlockSpec((tq, V_HEAD_DIM), lambda b, h, i, *_: (b * nq + i, h)),
        scratch_shapes=[pltpu.VMEM((2 * tq, 1), F32), pltpu.VMEM((2 * tq, 1), F32),
                        pltpu.VMEM((2 * tq, V_HEAD_DIM), F32), pltpu.VMEM((2 * tq, LANES), F32)],
    )
    return pl.pallas_call(
        kernel,
        grid_spec=grid_spec,
        out_shape=jax.ShapeDtypeStruct((n, attn_width), BF16),
        compiler_params=_params("arbitrary", "arbitrary", "arbitrary"),
        name="diff_attn",
    )(slopes2, band, fast, kmax.reshape(-1), qkv, qkv, qkv,
      vec(lq1), vec(lk1), vec(lq2), vec(lk2), vec(subln_g))


def _gates_kernel(cur_ref, prev_ref, next_ref, cw_ref, cb_ref, wg_ref, bg_ref, lam_ref,
                  hf_ref, ab_ref, ub_ref, ext_sc, af_ref, uf_ref, carry_sc, *, t, n_blocks, blk):
    j = pl.program_id(1)
    nt = pl.num_programs(1)
    halo = SUBLANES
    ext_sc[0:halo, :] = jnp.where(j > 0, prev_ref[...], 0.0)
    ext_sc[halo:halo + t, :] = cur_ref[...]
    ext_sc[halo + t:2 * halo + t, :] = jnp.where(j < nt - 1, next_ref[...], 0.0)
    left = CONV_WIDTH // 2
    y = cb_ref[...]
    for c in range(CONV_WIDTH):
        y = y + ext_sc[halo - left + c:halo - left + c + t, :] * cw_ref[c:c + 1, :]
    nl = -lam_ref[...]
    sp = jnp.maximum(nl, 0.0) + jnp.log1p(jnp.exp(-jnp.abs(nl)))
    outs = ((af_ref, uf_ref), (ab_ref, ub_ref))
    for g in range(n_blocks):
        cs = slice(g * blk, (g + 1) * blk)
        xc = y[:, cs]
        pre = jnp.dot(xc.astype(BF16), wg_ref[g], preferred_element_type=F32) + bg_ref[g:g + 1, :]
        for d in range(2):
            r = jax.nn.sigmoid(pre[:, d * blk:(d + 1) * blk])
            i = jax.nn.sigmoid(pre[:, (2 + d) * blk:(3 + d) * blk])
            log_a = (-RG_C * r) * sp[d:d + 1, cs]
            a = jnp.exp(log_a)
            u = jnp.sqrt(1.0 - jnp.exp(2.0 * log_a)) * (i * xc)
            outs[d][0][:, cs] = a
            outs[d][1][:, cs] = u

    @pl.when(j == 0)
    def _():
        carry_sc[...] = jnp.zeros(carry_sc.shape, F32)

    def step(i, h):
        h = af_ref[pl.ds(i, 1), :] * h + uf_ref[pl.ds(i, 1), :]
        hf_ref[pl.ds(i, 1), :] = h
        return h

    carry_sc[...] = lax.fori_loop(0, t, step, carry_sc[...], unroll=8)


def _rg_gates(xg, batch, seq, conv_w, conv_b, wg, bg, lam):
    n = batch * seq
    c = xg.shape[1] // 2
    n_blocks, blk = wg.shape[0], wg.shape[1]
    t = _tile(seq, 256)
    nt = seq // t
    hb = t // SUBLANES
    last = n // SUBLANES - 1
    kernel = functools.partial(_gates_kernel, t=t, n_blocks=n_blocks, blk=blk)
    full = lambda shape: pl.BlockSpec(shape, lambda b, j: (0,) * len(shape))
    out_spec = pl.BlockSpec((t, c), lambda b, j: (b * nt + j, 0))
    return pl.pallas_call(
        kernel,
        grid=(batch, nt),
        in_specs=[pl.BlockSpec((t, c), lambda b, j: (b * nt + j, 0)),
                  pl.BlockSpec((SUBLANES, c), lambda b, j: (jnp.maximum((b * nt + j) * hb - 1, 0), 0)),
                  pl.BlockSpec((SUBLANES, c), lambda b, j: (jnp.minimum((b * nt + j + 1) * hb, last), 0)),
                  full((CONV_WIDTH, c)), full((1, c)),
                  full(wg.shape), full(bg.shape), full((2, c))],
        out_specs=[out_spec] * 3,
        out_shape=[jax.ShapeDtypeStruct((n, c), F32)] * 3,
        scratch_shapes=[pltpu.VMEM((t + 2 * SUBLANES, c), F32), pltpu.VMEM((t, c), F32),
                        pltpu.VMEM((t, c), F32), pltpu.VMEM((1, c), F32)],
        compiler_params=_params("arbitrary", "arbitrary"),
        name="rg_gates",
    )(xg, xg, xg, conv_w, conv_b.reshape(1, c), wg, bg, lam)


def _scan_bwd_kernel(a_ref, u_ref, hf_ref, gr_ref, g_ref, o_ref, carry_sc, hb_sc, *, t):
    @pl.when(pl.program_id(1) == 0)
    def _():
        carry_sc[...] = jnp.zeros(carry_sc.shape, F32)

    def body(i, h):
        r = t - 1 - i
        h = a_ref[pl.ds(r, 1), :] * h + u_ref[pl.ds(r, 1), :]
        hb_sc[pl.ds(r, 1), :] = h
        return h

    carry_sc[...] = lax.fori_loop(0, t, body, carry_sc[...], unroll=8)
    rec = (hf_ref[...] + hb_sc[...]) * jax.nn.gelu(gr_ref[...], approximate=True)
    inv = lax.rsqrt(jnp.mean(rec * rec, axis=-1, keepdims=True) + NORM_EPS)
    o_ref[...] = (rec * inv * g_ref[...]).astype(o_ref.dtype)


def _scan_bwd_finish(a, u, hf, xg, norm_g, batch, seq):
    n, c = a.shape
    t = _tile(seq, 256)
    nt = seq // t
    rev = lambda b, j: (b * nt + (nt - 1 - j), 0)
    spec = pl.BlockSpec((t, c), rev)
    return pl.pallas_call(
        functools.partial(_scan_bwd_kernel, t=t),
        grid=(batch, nt),
        in_specs=[spec, spec, spec,
                  pl.BlockSpec((t, c), lambda b, j: (b * nt + (nt - 1 - j), 1)),
                  pl.BlockSpec((1, c), lambda b, j: (0, 0))],
        out_specs=spec,
        out_shape=jax.ShapeDtypeStruct((n, c), BF16),
        scratch_shapes=[pltpu.VMEM((1, c), F32), pltpu.VMEM((t, c), F32)],
        compiler_params=_params("arbitrary", "arbitrary"),
        name="scan_bwd",
    )(a, u, hf, xg, norm_g.reshape(1, c))


def _router_kernel(h_ref, g_ref, rw_ref, hn_ref, aff_ref, *, n_experts):
    x = h_ref[...]
    inv = lax.rsqrt(jnp.mean(x * x, axis=-1, keepdims=True) + NORM_EPS)
    hn = x * inv * g_ref[...]
    hn_ref[...] = hn
    logits = jnp.dot(hn.astype(BF16), rw_ref[...], preferred_element_type=F32)
    lane = lax.broadcasted_iota(jnp.int32, logits.shape, 1)
    logits = jnp.where(lane < n_experts, logits, -jnp.inf)
    e = jnp.exp(logits - jnp.max(logits, axis=-1, keepdims=True))
    aff_ref[...] = e / jnp.sum(e, axis=-1, keepdims=True)


def _router(h, g, rw_pad, n_experts):
    n, d = h.shape
    tm = _tile(n, 256)
    return pl.pallas_call(
        functools.partial(_router_kernel, n_experts=n_experts),
        grid=(n // tm,),
        in_specs=[pl.BlockSpec((tm, d), lambda i: (i, 0)),
                  pl.BlockSpec((1, d), lambda i: (0, 0)),
                  pl.BlockSpec((d, LANES), lambda i: (0, 0))],
        out_specs=[pl.BlockSpec((tm, d), lambda i: (i, 0)),
                   pl.BlockSpec((tm, LANES), lambda i: (i, 0))],
        out_shape=[jax.ShapeDtypeStruct((n, d), F32), jax.ShapeDtypeStruct((n, LANES), F32)],
        compiler_params=_params("arbitrary"),
        name="router",
    )(h, g.reshape(1, d), rw_pad)


def _route_kernel(aff_ref, dest_ref, *, cap, n_tok, chunk):
    n_chunks = n_tok // chunk

    def bits_of(c):
        return lax.bitcast_convert_type(aff_ref[pl.ds(pl.multiple_of(c * chunk, chunk), chunk), :],
                                        jnp.int32)

    def count(pred):
        def body(c, acc):
            return acc + jnp.sum(jnp.where(pred(bits_of(c)), 1.0, 0.0), axis=0, keepdims=True)
        return lax.fori_loop(0, n_chunks, body, jnp.zeros((1, LANES), F32))

    def search(i, thr):
        cand = thr | lax.shift_left(jnp.int32(1), 30 - i)
        return jnp.where(count(lambda v: v >= cand) >= cap, cand, thr)

    thr = lax.fori_loop(0, 31, search, jnp.zeros((1, LANES), jnp.int32))
    need = cap - count(lambda v: v > thr)

    ltri = jnp.where(lax.broadcasted_iota(jnp.int32, (chunk, chunk), 0)
                     > lax.broadcasted_iota(jnp.int32, (chunk, chunk), 1), 1.0, 0.0).astype(BF16)

    def place(c, carry):
        ties_before, taken_before = carry
        v = bits_of(c)
        tie = v == thr
        tie_f = jnp.where(tie, 1.0, 0.0)
        tie_rank = jnp.dot(ltri, tie_f.astype(BF16), preferred_element_type=F32) + ties_before
        take = jnp.logical_or(v > thr, jnp.logical_and(tie, tie_rank < need))
        take_f = jnp.where(take, 1.0, 0.0)
        pos = jnp.dot(ltri, take_f.astype(BF16), preferred_element_type=F32) + taken_before
        dest_ref[pl.ds(pl.multiple_of(c * chunk, chunk), chunk), :] = jnp.where(
            take, pos.astype(jnp.int32), cap)
        return (ties_before + jnp.sum(tie_f, axis=0, keepdims=True),
                taken_before + jnp.sum(take_f, axis=0, keepdims=True))

    zero = jnp.zeros((1, LANES), F32)
    lax.fori_loop(0, n_chunks, place, (zero, zero))


def _route(aff_pad, cap):
    n_tok = aff_pad.shape[0]
    chunk = _tile(n_tok, LANES)
    return pl.pallas_call(
        functools.partial(_route_kernel, cap=cap, n_tok=n_tok, chunk=chunk),
        out_shape=jax.ShapeDtypeStruct((n_tok, LANES), jnp.int32),
        compiler_params=pltpu.CompilerParams(vmem_limit_bytes=VMEM_LIMIT_BYTES),
        name="route",
    )(aff_pad)


def _compact_kernel(dest_ref, bits_ref, idx_ref, gate_ref, *, n_tok, cap):
    def clear(i, c):
        idx_ref[i] = 0
        gate_ref[i] = 0
        return c

    lax.fori_loop(cap, idx_ref.shape[0], clear, 0)

    def body(n, c):
        idx_ref[dest_ref[n]] = n
        return c

    lax.fori_loop(0, n_tok, body, 0, unroll=16)

    def gate(s, c):
        gate_ref[s] = bits_ref[idx_ref[s]]
        return c

    lax.fori_loop(0, cap, gate, 0, unroll=16)


def _compact(dest_flat, bits_flat, n_experts, n_tok, cap):
    width = cap + SMEM_TILE_1D
    in_spec = pl.BlockSpec((n_tok,), lambda e: (e,), memory_space=pltpu.SMEM)
    out_spec = pl.BlockSpec((width,), lambda e: (e,), memory_space=pltpu.SMEM)
    out = jax.ShapeDtypeStruct((n_experts * width,), jnp.int32)
    idx, gate = pl.pallas_call(
        functools.partial(_compact_kernel, n_tok=n_tok, cap=cap),
        grid=(n_experts,),
        in_specs=[in_spec, in_spec],
        out_specs=[out_spec, out_spec],
        out_shape=[out, out],
        compiler_params=_params("arbitrary"),
        name="compact",
    )(dest_flat, bits_flat)
    return idx.reshape(n_experts, width), gate.reshape(n_experts, width)


def _row_copy(src, dst, src_row, dst_row, sem):
    return pltpu.make_async_copy(src.at[pl.ds(src_row, 1), :], dst.at[pl.ds(dst_row, 1), :], sem)


def _tile_wait(src, dst, sem, rows):
    pltpu.make_async_copy(src.at[pl.ds(0, rows), :], dst.at[pl.ds(0, rows), :], sem).wait()


def _gather_kernel(idx_ref, src_ref, o_ref, buf, sem, *, tg):
    i = pl.program_id(0)
    n = pl.num_programs(0)

    def issue(tile, slot):
        def start(r, c):
            _row_copy(src_ref, buf.at[slot], idx_ref[tile * tg + r], r, sem.at[slot]).start()
            return c
        lax.fori_loop(0, tg, start, 0, unroll=8)

    @pl.when(i == 0)
    def _():
        issue(0, 0)

    @pl.when(i + 1 < n)
    def _():
        issue(i + 1, (i + 1) % 2)

    slot = i % 2
    _tile_wait(src_ref, buf.at[slot], sem.at[slot], tg)
    o_ref[...] = buf[slot].astype(o_ref.dtype)


def _gather_rows(src, idx, out_dtype):
    n_slots = idx.shape[0]
    d = src.shape[1]
    tg = _tile(n_slots, GATHER_ROWS)
    grid_spec = pltpu.PrefetchScalarGridSpec(
        num_scalar_prefetch=1,
        grid=(n_slots // tg,),
        in_specs=[pl.BlockSpec(memory_space=pl.ANY)],
        out_specs=pl.BlockSpec((tg, d), lambda i, s: (i, 0)),
        scratch_shapes=[pltpu.VMEM((2, tg, d), src.dtype), pltpu.SemaphoreType.DMA((2,))],
    )
    return pl.pallas_call(
        functools.partial(_gather_kernel, tg=tg),
        grid_spec=grid_spec,
        out_shape=jax.ShapeDtypeStruct((n_slots, d), out_dtype),
        compiler_params=_params("arbitrary"),
        name="gather_rows",
    )(idx, src)


SCATTER_SLOTS = 3


def _scatter_add_kernel(idx_ref, ye_ref, h_in_ref, h_ref, buf, rsem, wsem, *, tg, tiles_per_expert):
    del h_in_ref
    i = pl.program_id(0)
    first = i % tiles_per_expert == 0
    last = i % tiles_per_expert == tiles_per_expert - 1

    def read(tile):
        slot = tile % SCATTER_SLOTS

        def start(r, c):
            _row_copy(h_ref, buf.at[slot], idx_ref[tile * tg + r], r, rsem.at[slot]).start()
            return c
        lax.fori_loop(0, tg, start, 0, unroll=8)

    def write(tile):
        slot = tile % SCATTER_SLOTS

        def start(r, c):
            _row_copy(buf.at[slot], h_ref, r, idx_ref[tile * tg + r], wsem.at[slot]).start()
            return c
        lax.fori_loop(0, tg, start, 0, unroll=8)

    def wait_write(tile):
        slot = tile % SCATTER_SLOTS
        _tile_wait(buf.at[slot], h_ref, wsem.at[slot], tg)

    @pl.when(first)
    def _():
        read(i)

    @pl.when(jnp.logical_not(last))
    def _():
        read(i + 1)

    slot = i % SCATTER_SLOTS
    _tile_wait(h_ref, buf.at[slot], rsem.at[slot], tg)
    buf[slot] = buf[slot] + ye_ref[...]
    write(i)

    @pl.when(jnp.logical_not(first))
    def _():
        wait_write(i - 1)

    @pl.when(last)
    def _():
        wait_write(i)


def _scatter_add(h, ye, idx, cap):
    n_slots, d = ye.shape
    tg = _tile(cap, SCATTER_ROWS)
    grid_spec = pltpu.PrefetchScalarGridSpec(
        num_scalar_prefetch=1,
        grid=(n_slots // tg,),
        in_specs=[pl.BlockSpec((tg, d), lambda i, s: (i, 0)),
                  pl.BlockSpec(memory_space=pl.ANY)],
        out_specs=pl.BlockSpec(memory_space=pl.ANY),
        scratch_shapes=[pltpu.VMEM((SCATTER_SLOTS, tg, d), F32),
                        pltpu.SemaphoreType.DMA((SCATTER_SLOTS,)),
                        pltpu.SemaphoreType.DMA((SCATTER_SLOTS,))],
    )
    return pl.pallas_call(
        functools.partial(_scatter_add_kernel, tg=tg, tiles_per_expert=cap // tg),
        grid_spec=grid_spec,
        out_shape=jax.ShapeDtypeStruct(h.shape, h.dtype),
        input_output_aliases={2: 0},
        compiler_params=_params("arbitrary"),
        name="scatter_add",
    )(idx, ye, h)


def _ffn_up_kernel(x_ref, w1_ref, w3_ref, o_ref):
    x = x_ref[...]
    a = jnp.dot(x, w1_ref[...].astype(BF16), preferred_element_type=F32)
    b = jnp.dot(x, w3_ref[...].astype(BF16), preferred_element_type=F32)
    o_ref[...] = (jax.nn.silu(a) * b).astype(o_ref.dtype)


def _ffn_up(xe, w1, w3, cap, tn=256):
    n_experts, d, f = w1.shape
    tn = _tile(f, tn)
    wspec = pl.BlockSpec((None, d, tn), lambda e, j: (e, 0, j))
    return pl.pallas_call(
        _ffn_up_kernel,
        grid=(n_experts, f // tn),
        in_specs=[pl.BlockSpec((cap, d), lambda e, j: (e, 0), pipeline_mode=pl.Buffered(1)),
                  wspec, wspec],
        out_specs=pl.BlockSpec((cap, tn), lambda e, j: (e, j)),
        out_shape=jax.ShapeDtypeStruct((n_experts * cap, f), BF16),
        compiler_params=_params("arbitrary", "arbitrary"),
        name="ffn_up",
    )(xe, w1, w3)


def _ffn_down_kernel(h_ref, w2_ref, gate_ref, o_ref):
    acc = jnp.dot(h_ref[...], w2_ref[...].astype(BF16), preferred_element_type=F32)
    o_ref[...] = acc * gate_ref[...]


def _ffn_down(hm, w2, gates, cap, tn=256):
    n_experts, f, d = w2.shape
    tn = _tile(d, tn)
    return pl.pallas_call(
        _ffn_down_kernel,
        grid=(n_experts, d // tn),
        in_specs=[pl.BlockSpec((cap, f), lambda e, j: (e, 0), pipeline_mode=pl.Buffered(1)),
                  pl.BlockSpec((None, f, tn), lambda e, j: (e, 0, j)),
                  pl.BlockSpec((cap, 1), lambda e, j: (e, 0), pipeline_mode=pl.Buffered(1))],
        out_specs=pl.BlockSpec((cap, tn), lambda e, j: (e, j)),
        out_shape=jax.ShapeDtypeStruct((n_experts * cap, d), F32),
        compiler_params=_params("arbitrary", "arbitrary"),
        name="ffn_down",
    )(hm, w2, gates)


def _prepare_weights(rg_wa, rg_ba, rg_wx, rg_bx, router_w):
    n_blocks, blk = rg_wa.shape[1], rg_wa.shape[2]
    wg = jnp.concatenate([rg_wa[0], rg_wa[1], rg_wx[0], rg_wx[1]], axis=-1).astype(BF16)
    bg = jnp.concatenate([rg_ba[0].reshape(n_blocks, blk), rg_ba[1].reshape(n_blocks, blk),
                          rg_bx[0].reshape(n_blocks, blk), rg_bx[1].reshape(n_blocks, blk)], axis=-1)
    n_experts = router_w.shape[1]
    rw_pad = jnp.pad(router_w, ((0, 0), (0, LANES - n_experts))).astype(BF16)
    return dict(wg=wg, bg=bg, rw_pad=rw_pad)


def _layer(x, layer, pw, norm_mix_g, w_in, lambda_q1, lambda_k1, lambda_q2, lambda_k2,
           attn_subln_g, conv_w, conv_b, rg_lambda, rec_norm_g, w_out, norm_ffn_g,
           exp_w1, exp_w3, exp_w2, n_experts, attn_width):
    batch, seq, d = x.shape
    n = batch * seq
    n_heads = attn_width // V_HEAD_DIM
    rec_width = conv_w.shape[-1]
    x2 = x.reshape(n, d)

    xn = _rmsnorm(x2, norm_mix_g, BF16)
    q_scale = QK_HALF_DIM ** -0.5 * LOG2E
    qkv_scale = jnp.concatenate([jnp.full((1, attn_width), q_scale, F32),
                                 jnp.ones((1, 2 * attn_width), F32)], axis=1)
    qkv = _matmul(xn, w_in, 0, 3 * attn_width, qkv_scale, BF16)
    xg = _matmul(xn, w_in, 3 * attn_width, 2 * rec_width, jnp.ones((1, 2 * rec_width), F32), F32)
    lam_init = 0.8 - 0.6 * math.exp(-0.3 * layer)
    slopes = 2.0 ** (-8.0 * jnp.arange(1, n_heads + 1, dtype=F32) / n_heads)
    attn = _diff_attention(qkv, batch, seq, n_heads, slopes, lambda_q1, lambda_k1, lambda_q2,
                           lambda_k2, attn_subln_g, lam_init)
    h_f, a_b, u_b = _rg_gates(xg, batch, seq, conv_w, conv_b, pw["wg"], pw["bg"], rg_lambda)
    rec = _scan_bwd_finish(a_b, u_b, h_f, xg, rec_norm_g, batch, seq)
    h = _outproj(attn, rec, w_out, x2)

    cap = (EC_CAPACITY_FACTOR * n) // n_experts
    hn, aff = _router(h, norm_ffn_g, pw["rw_pad"], n_experts)
    dest = _route(aff, cap)
    bits_t = lax.bitcast_convert_type(aff[:, :n_experts].T, jnp.int32)
    idx, gate_bits = _compact(dest[:, :n_experts].T.reshape(-1), bits_t.reshape(-1),
                              n_experts, n, cap)
    idx = idx[:, :cap].reshape(-1)
    gates = lax.bitcast_convert_type(gate_bits[:, :cap], F32).reshape(-1, 1)
    xe = _gather_rows(hn, idx, BF16)
    hm = _ffn_up(xe, exp_w1, exp_w3, cap)
    ye = _ffn_down(hm, exp_w2, gates, cap)
    return _scatter_add(h, ye, idx, cap)


def _trunk(x, pw, norm_mix_g, w_in, lambda_q1, lambda_k1, lambda_q2, lambda_k2, attn_subln_g,
           conv_w, conv_b, rg_lambda, rec_norm_g, w_out, norm_ffn_g, exp_w1, exp_w3, exp_w2,
           final_norm_g, n_experts, attn_width):
    batch, seq, d = x.shape
    h = x
    for l in range(norm_mix_g.shape[0]):
        h = _layer(h.reshape(batch, seq, d), l, pw[l], norm_mix_g[l], w_in[l], lambda_q1[l],
                   lambda_k1[l], lambda_q2[l], lambda_k2[l], attn_subln_g[l], conv_w[l], conv_b[l],
                   rg_lambda[l], rec_norm_g[l], w_out[l], norm_ffn_g[l], exp_w1[l], exp_w3[l],
                   exp_w2[l], n_experts, attn_width)
    return _rmsnorm(h.reshape(batch * seq, d), final_norm_g, F32).reshape(batch, seq, d)


def kernel(x_prompt, x_sample, norm_mix_g, w_in, lambda_q1, lambda_k1, lambda_q2, lambda_k2, attn_subln_g, conv_w, conv_b, rg_wa, rg_ba, rg_wx, rg_bx, rg_lambda, rec_norm_g, w_out, norm_ffn_g, router_w, exp_w1, exp_w3, exp_w2, final_norm_g):
    depth = w_in.shape[0]
    rec_width = conv_w.shape[-1]
    attn_width = (w_in.shape[-1] - 2 * rec_width) // 3
    n_experts = router_w.shape[-1]
    pw = [_prepare_weights(rg_wa[l], rg_ba[l], rg_wx[l], rg_bx[l], router_w[l]) for l in range(depth)]
    run = lambda x: _trunk(x, pw, norm_mix_g, w_in, lambda_q1, lambda_k1, lambda_q2, lambda_k2,
                           attn_subln_g, conv_w, conv_b, rg_lambda, rec_norm_g, w_out, norm_ffn_g,
                           exp_w1, exp_w3, exp_w2, final_norm_g, n_experts, attn_width)
    return (run(x_prompt), run(x_sample))
```
